```python
import functools
import jax, jax.numpy as jnp
from jax import lax
import numpy as np

D_MODEL = 1024
BATCH = 2
SEQ = 8192
DEPTH = 2
DEC_BATCH = 32
DEC_SEQ = 1
PAST_LEN = 16384
PAGE_SIZE = 128

N_HEADS = 8
HEAD_DIM = 64
D_ATT = N_HEADS * HEAD_DIM
POOL_WINDOWS = (2, 4, 8, 16)
N_POOL_GROUPS = len(POOL_WINDOWS)
D_POOL = D_MODEL // 4
POOL_GROUP = D_POOL // N_POOL_GROUPS
POOL_BUF = max(POOL_WINDOWS) - 1
D_CONV = D_MODEL // 4
CONV_WIDTH = 31
CONV_BUF = CONV_WIDTH - 1
N_BRANCHES = 3
D_FF = 4 * D_MODEL
Q_BLOCK = 128
NORM_EPS = 1e-6
LN_EPS = 1e-5
IN_SPLITS = (D_ATT, D_ATT, D_ATT, N_HEADS, D_POOL, 2 * D_CONV, N_BRANCHES * D_MODEL)
D_IN = sum(IN_SPLITS)
SPLIT_POINTS = tuple(sum(IN_SPLITS[:i + 1]) for i in range(len(IN_SPLITS) - 1))

kernel_name = "hybrid_pool_fox_conformer_decode_step"


def rms_norm(x, g):
    xf = x.astype(jnp.float32)
    y = xf * lax.rsqrt(jnp.mean(xf * xf, axis=-1, keepdims=True) + NORM_EPS)
    return (y * g.astype(jnp.float32)).astype(x.dtype)


def layer_norm(x, g, b):
    xf = x.astype(jnp.float32)
    mu = jnp.mean(xf, axis=-1, keepdims=True)
    var = jnp.mean(jnp.square(xf - mu), axis=-1, keepdims=True)
    y = (xf - mu) * lax.rsqrt(var + LN_EPS)
    return (y * g.astype(jnp.float32) + b.astype(jnp.float32)).astype(x.dtype)


def forgetting_attention_prompt(q, k, v, logf):
    B, T, H, Dh = q.shape
    c = jnp.cumsum(logf, axis=1)
    c_k = c.transpose(0, 2, 1)
    nb = T // Q_BLOCK
    q_blocks = q.reshape(B, nb, Q_BLOCK, H, Dh).transpose(1, 0, 2, 3, 4)
    c_blocks = c.reshape(B, nb, Q_BLOCK, H).transpose(1, 0, 3, 2)
    starts = jnp.arange(nb, dtype=jnp.int32) * Q_BLOCK
    k_pos = jnp.arange(T, dtype=jnp.int32)
    scale = HEAD_DIM ** -0.5

    def block(args):
        qb, cb, s0 = args
        logits = jnp.einsum('bqhd,bkhd->bhqk', qb, k, preferred_element_type=jnp.float32) * scale
        logits = logits + (cb[..., :, None] - c_k[:, :, None, :])
        q_pos = s0 + jnp.arange(Q_BLOCK, dtype=jnp.int32)
        mask = k_pos[None, :] <= q_pos[:, None]
        logits = jnp.where(mask[None, None], logits, -jnp.inf)
        p = jax.nn.softmax(logits, axis=-1)
        return jnp.einsum('bhqk,bkhd->bqhd', p.astype(v.dtype), v)

    out = lax.map(block, (q_blocks, c_blocks, starts))
    return out.transpose(1, 0, 2, 3, 4).reshape(B, T, H * Dh)


def forgetting_attention_sample(q, k_new, v_new, logf_new, cache_k_l, cache_v_l, cache_logf_l, page_table):
    Bd, S, H, Dh = q.shape
    P = page_table.shape[1] * PAGE_SIZE
    k_past = cache_k_l[page_table].reshape(Bd, P, H, Dh).astype(k_new.dtype)
    v_past = cache_v_l[page_table].reshape(Bd, P, H, Dh).astype(v_new.dtype)
    f_past = cache_logf_l[page_table].reshape(Bd, P, H).astype(jnp.float32)
    k_all = jnp.concatenate([k_past, k_new], axis=1)
    v_all = jnp.concatenate([v_past, v_new], axis=1)
    c = jnp.cumsum(jnp.concatenate([f_past, logf_new], axis=1), axis=1).transpose(0, 2, 1)
    c_q = c[:, :, P:]
    scale = HEAD_DIM ** -0.5
    logits = jnp.einsum('bqhd,bkhd->bhqk', q, k_all, preferred_element_type=jnp.float32) * scale
    logits = logits + (c_q[..., :, None] - c[..., None, :])
    k_pos = jnp.arange(P + S, dtype=jnp.int32)
    q_pos = P + jnp.arange(S, dtype=jnp.int32)
    mask = k_pos[None, :] <= q_pos[:, None]
    logits = jnp.where(mask[None, None], logits, -jnp.inf)
    p = jax.nn.softmax(logits, axis=-1)
    out = jnp.einsum('bhqk,bkhd->bqhd', p.astype(v_all.dtype), v_all)
    return out.reshape(Bd, S, H * Dh)


def pool_mixer(u_new, prefix, start_pos, pool_w_l, pool_scale_l, w_pool_up_l):
    B, T, _ = u_new.shape
    u_full = jnp.concatenate([prefix.astype(u_new.dtype), u_new], axis=1)
    uf = u_full.astype(jnp.float32)
    csum = jnp.cumsum(jnp.pad(uf, ((0, 0), (1, 0), (0, 0))), axis=1)
    hi = csum[:, POOL_BUF + 1:]
    pos = start_pos + jnp.arange(T, dtype=jnp.int32)
    means = []
    for g, w in enumerate(POOL_WINDOWS):
        cs = slice(g * POOL_GROUP, (g + 1) * POOL_GROUP)
        lo = csum[:, POOL_BUF + 1 - w:POOL_BUF + 1 - w + T, cs]
        count = jnp.minimum(pos + 1, w).astype(jnp.float32)[None, :, None]
        means.append((hi[..., cs] - lo) / count)
    mixed = (jnp.concatenate(means, axis=-1) - uf[:, POOL_BUF:]).astype(u_new.dtype)
    mixed = mixed.reshape(B, T, N_POOL_GROUPS, POOL_GROUP)
    mixed = jnp.einsum('btgc,gcd->btgd', mixed, pool_w_l).reshape(B, T, D_POOL) * pool_scale_l
    return mixed @ w_pool_up_l, u_full[:, -POOL_BUF:]


def conv_mixer(glu_new, prefix, dw_w_l, dw_b_l, ln_g_l, ln_b_l, w_conv_out_l):
    ctx = jnp.concatenate([prefix.astype(glu_new.dtype), glu_new], axis=1)
    y = lax.conv_general_dilated(ctx, dw_w_l[:, None, :].astype(ctx.dtype), window_strides=(1,),
                                 padding='VALID', dimension_numbers=('NWC', 'WIO', 'NWC'),
                                 feature_group_count=D_CONV) + dw_b_l
    y = jax.nn.silu(layer_norm(y, ln_g_l, ln_b_l))
    return y @ w_conv_out_l, ctx[:, -CONV_BUF:]


def trunk_layer(x, params, attend, pool_prefix, conv_prefix, start_pos):
    (n1, w_in_l, b_f_l, b_gate_l, qg, kg, pool_w_l, pool_scale_l, w_pool_up_l, w_att_up_l,
     dw_w_l, dw_b_l, ln_g_l, ln_b_l, w_conv_out_l, w_out_l, n2, w_up_l, w_down_l) = params
    B, T, _ = x.shape
    h = rms_norm(x, n1)
    z = h @ w_in_l
    q, k, v, fl, u_pool, u_conv, gl = jnp.split(z, SPLIT_POINTS, axis=-1)
    q = rms_norm(q.reshape(B, T, N_HEADS, HEAD_DIM), qg)
    k = rms_norm(k.reshape(B, T, N_HEADS, HEAD_DIM), kg)
    v = v.reshape(B, T, N_HEADS, HEAD_DIM)
    logf = jax.nn.log_sigmoid((fl + b_f_l).astype(jnp.float32))
    glu = u_conv[..., :D_CONV] * jax.nn.sigmoid(u_conv[..., D_CONV:])
    gates = jax.nn.sigmoid(gl.reshape(B, T, N_BRANCHES, D_MODEL) + b_gate_l)

    att = attend(q, k, v, logf) @ w_att_up_l
    pool, pool_state = pool_mixer(u_pool, pool_prefix, start_pos, pool_w_l, pool_scale_l, w_pool_up_l)
    conv, conv_state = conv_mixer(glu, conv_prefix, dw_w_l, dw_b_l, ln_g_l, ln_b_l, w_conv_out_l)

    merged = jnp.sum(gates * jnp.stack([att, pool, conv], axis=2), axis=2)
    x = x + merged @ w_out_l
    hid = jnp.square(jax.nn.relu(rms_norm(x, n2) @ w_up_l))
    x = x + hid @ w_down_l
    return x, k, v, logf, pool_state, conv_state


def setup_inputs(seed: int = 0) -> dict:
    key = jax.random.key(seed)
    ks = jax.random.split(key, 32)
    f32 = jnp.float32
    n_pages = PAST_LEN // PAGE_SIZE
    n_used = DEC_BATCH * n_pages
    n_pool_pages = n_used + max(1, n_used // 4)

    def nrm(k, shape, scale):
        return jax.random.normal(k, shape, f32) * scale

    x_prompt = nrm(ks[0], (BATCH, SEQ, D_MODEL), 1.0)
    x_sample = nrm(ks[1], (DEC_BATCH, DEC_SEQ, D_MODEL), 1.0)
    cache_k = nrm(ks[2], (DEPTH, n_pool_pages, PAGE_SIZE, N_HEADS, HEAD_DIM), 1.0)
    cache_v = nrm(ks[3], (DEPTH, n_pool_pages, PAGE_SIZE, N_HEADS, HEAD_DIM), 1.0)
    cache_logf = jax.nn.log_sigmoid(3.0 + 0.5 * jax.random.normal(ks[4], (DEPTH, n_pool_pages, PAGE_SIZE, N_HEADS), f32))
    state_pool = nrm(ks[5], (DEPTH, DEC_BATCH, POOL_BUF, D_POOL), 1.0)
    state_conv = nrm(ks[6], (DEPTH, DEC_BATCH, CONV_BUF, D_CONV), 0.5)
    page_table = jax.random.permutation(ks[7], n_pool_pages)[:n_used].reshape(DEC_BATCH, n_pages).astype(jnp.int32)

    return {
        "x_prompt": x_prompt,
        "x_sample": x_sample,
        "cache_k": cache_k,
        "cache_v": cache_v,
        "cache_logf": cache_logf,
        "state_pool": state_pool,
        "state_conv": state_conv,
        "page_table": page_table,
        "norm1_g": 1.0 + nrm(ks[8], (DEPTH, D_MODEL), 0.05),
        "w_in": nrm(ks[9], (DEPTH, D_MODEL, D_IN), D_MODEL ** -0.5),
        "b_f": 3.0 + nrm(ks[10], (DEPTH, N_HEADS), 0.5),
        "b_gate": nrm(ks[11], (DEPTH, N_BRANCHES, D_MODEL), 0.02),
        "q_gain": 1.0 + nrm(ks[12], (DEPTH, HEAD_DIM), 0.05),
        "k_gain": 1.0 + nrm(ks[13], (DEPTH, HEAD_DIM), 0.05),
        "pool_w": nrm(ks[14], (DEPTH, N_POOL_GROUPS, POOL_GROUP, POOL_GROUP), POOL_GROUP ** -0.5),
        "pool_scale": 1.0 + nrm(ks[15], (DEPTH, D_POOL), 0.1),
        "w_pool_up": nrm(ks[16], (DEPTH, D_POOL, D_MODEL), D_POOL ** -0.5),
        "w_att_up": nrm(ks[17], (DEPTH, D_ATT, D_MODEL), D_ATT ** -0.5),
        "dw_w": nrm(ks[18], (DEPTH, CONV_WIDTH, D_CONV), CONV_WIDTH ** -0.5),
        "dw_b": nrm(ks[19], (DEPTH, D_CONV), 0.02),
        "conv_ln_g": 1.0 + nrm(ks[20], (DEPTH, D_CONV), 0.05),
        "conv_ln_b": nrm(ks[21], (DEPTH, D_CONV), 0.02),
        "w_conv_out": nrm(ks[22], (DEPTH, D_CONV, D_MODEL), D_CONV ** -0.5),
        "w_out": nrm(ks[23], (DEPTH, D_MODEL, D_MODEL), D_MODEL ** -0.5),
        "norm2_g": 1.0 + nrm(ks[24], (DEPTH, D_MODEL), 0.05),
        "w_mlp_up": nrm(ks[25], (DEPTH, D_MODEL, D_FF), D_MODEL ** -0.5),
        "w_mlp_down": nrm(ks[26], (DEPTH, D_FF, D_MODEL), D_FF ** -0.5),
    }


def reference(x_prompt, x_sample, cache_k, cache_v, cache_logf, state_pool, state_conv, page_table,
              norm1_g, w_in, b_f, b_gate, q_gain, k_gain, pool_w, pool_scale, w_pool_up, w_att_up,
              dw_w, dw_b, conv_ln_g, conv_ln_b, w_conv_out, w_out, norm2_g, w_mlp_up, w_mlp_down):
    past_len = page_table.shape[1] * PAGE_SIZE
    xp, xs = x_prompt, x_sample
    zeros_pool = jnp.zeros((xp.shape[0], POOL_BUF, D_POOL), xp.dtype)
    zeros_conv = jnp.zeros((xp.shape[0], CONV_BUF, D_CONV), xp.dtype)
    kp_l, vp_l, fp_l, pp_l, cp_l = [], [], [], [], []
    ks_l, vs_l, fs_l, ps_l, cs_l = [], [], [], [], []
    for l in range(DEPTH):
        params = (norm1_g[l], w_in[l], b_f[l], b_gate[l], q_gain[l], k_gain[l], pool_w[l], pool_scale[l],
                  w_pool_up[l], w_att_up[l], dw_w[l], dw_b[l], conv_ln_g[l], conv_ln_b[l], w_conv_out[l],
                  w_out[l], norm2_g[l], w_mlp_up[l], w_mlp_down[l])
        xp, kp, vp, fp, pp, cp = trunk_layer(xp, params, forgetting_attention_prompt, zeros_pool, zeros_conv, 0)
        attend_s = functools.partial(forgetting_attention_sample, cache_k_l=cache_k[l], cache_v_l=cache_v[l],
                                     cache_logf_l=cache_logf[l], page_table=page_table)
        xs, k_s, v_s, f_s, p_s, c_s = trunk_layer(xs, params, attend_s, state_pool[l], state_conv[l], past_len)
        kp_l.append(kp); vp_l.append(vp); fp_l.append(fp); pp_l.append(pp); cp_l.append(cp)
        ks_l.append(k_s); vs_l.append(v_s); fs_l.append(f_s); ps_l.append(p_s); cs_l.append(c_s)
    return (xp, xs,
            jnp.stack(kp_l), jnp.stack(vp_l), jnp.stack(fp_l), jnp.stack(pp_l), jnp.stack(cp_l),
            jnp.stack(ks_l), jnp.stack(vs_l), jnp.stack(fs_l), jnp.stack(ps_l), jnp.stack(cs_l))
```

```python
import functools

import jax
import jax.numpy as jnp
from jax import lax
from jax.experimental import pallas as pl
from jax.experimental.pallas import tpu as pltpu

F32 = jnp.float32
BF16 = jnp.bfloat16

N_HEADS = 8
HEAD_DIM = 64
D_ATT = N_HEADS * HEAD_DIM
N_PAIRS = N_HEADS // 2
POOL_WINDOWS = (2, 4, 8, 16)
POOL_BUF = max(POOL_WINDOWS) - 1
POOL_HIST = 16
CONV_WIDTH = 31
CONV_BUF = CONV_WIDTH - 1
CONV_HIST = 32
NORM_EPS = 1e-6
LN_EPS = 1e-5
LANES = 128
VMEM_LIMIT = 56 * 1024 * 1024

C_Q, C_K, C_V, C_POOL, C_CONV, C_F = 0, 512, 1024, 1536, 1792, 2304
D_MAIN = C_F + LANES


def _cparams(n_axes):
    return pltpu.CompilerParams(dimension_semantics=("arbitrary",) * n_axes, vmem_limit_bytes=VMEM_LIMIT)


def _split3(x):
    hi = x.astype(BF16).astype(F32)
    r = x - hi
    mid = r.astype(BF16).astype(F32)
    lo = (r - mid).astype(BF16).astype(F32)
    return hi, mid, lo


def _rms(x, g):
    return x * lax.rsqrt(jnp.mean(x * x, axis=-1, keepdims=True) + NORM_EPS) * g


def _log_sigmoid(x):
    return jnp.minimum(x, 0.0) - jnp.log1p(jnp.exp(-jnp.abs(x)))


def _head_norm(a, gsum, gain):
    outs = []
    for c in range(2):
        blk = a[:, c * 256:(c + 1) * 256]
        sq = blk * blk
        hi = sq.astype(BF16)
        lo = (sq - hi.astype(F32)).astype(BF16)
        ss = jnp.dot(hi, gsum, preferred_element_type=F32) + jnp.dot(lo, gsum, preferred_element_type=F32)
        outs.append(blk * lax.rsqrt(ss * (1.0 / HEAD_DIM) + NORM_EPS))
    return jnp.concatenate(outs, axis=1) * gain


def _pool_window_lanes(shape, col_block):
    lane = lax.broadcasted_iota(jnp.int32, shape, len(shape) - 1)
    lo_w, hi_w = POOL_WINDOWS[2 * col_block], POOL_WINDOWS[2 * col_block + 1]
    return jnp.where(lane < 64, lo_w, hi_w)


def _layer_norm_silu(y, g, b):
    mu = jnp.mean(y, axis=-1, keepdims=True)
    d = y - mu
    var = jnp.mean(d * d, axis=-1, keepdims=True)
    yn = d * lax.rsqrt(var + LN_EPS) * g + b
    return yn * jax.nn.sigmoid(yn)


def _inproj_prompt_kernel(x_ref, n1_ref, w_ref, bf_ref, qg_ref, kg_ref, gsum_ref, ltri_ref, poolw_ref,
                          pscale_ref, dww_ref, dwb_ref, lng_ref, lnb_ref,
                          qa_ref, ka_ref, vb_ref, k_ref, v_ref, logf_ref, pm_ref, ca_ref, ptail_ref, ctail_ref,
                          uh_ref, gh_ref, carry_ref, *, tm):
    t = pl.program_id(1)

    @pl.when(t == 0)
    def _():
        uh_ref[0:POOL_HIST, :] = jnp.zeros((POOL_HIST, 256), F32)
        gh_ref[0:CONV_HIST, :] = jnp.zeros((CONV_HIST, 256), F32)
        carry_ref[...] = jnp.zeros((1, LANES), F32)

    x = x_ref[0]
    h = _rms(x, n1_ref[...]).astype(BF16)
    z = jnp.dot(h, w_ref[...], preferred_element_type=F32)

    gsum = gsum_ref[...]
    qn = _head_norm(z[:, C_Q:C_Q + D_ATT], gsum, qg_ref[...]) * (HEAD_DIM ** -0.5)
    kn = _head_norm(z[:, C_K:C_K + D_ATT], gsum, kg_ref[...])
    v = z[:, C_V:C_V + D_ATT]
    k_ref[0] = kn
    v_ref[0] = v
    vb_ref[0] = v.astype(BF16)

    lane = lax.broadcasted_iota(jnp.int32, (tm, LANES), 1)
    logf = jnp.where(lane < N_HEADS, _log_sigmoid(z[:, C_F:C_F + LANES] + bf_ref[...]), 0.0)
    logf_ref[0] = logf[:, 0:N_HEADS]
    f_hi, f_mid, f_lo = _split3(logf)
    f3 = (f_hi + pltpu.roll(f_mid, 8, 1) + pltpu.roll(f_lo, 16, 1)).astype(BF16)
    cs3 = jnp.dot(ltri_ref[...], f3, preferred_element_type=F32)
    c = cs3 + pltpu.roll(cs3, LANES - 8, 1) + pltpu.roll(cs3, LANES - 16, 1) + carry_ref[...]
    c = jnp.where(lane < N_HEADS, c, 0.0)
    carry_ref[...] = c[tm - 1:tm, :]
    c_hi, c_mid, c_lo = _split3(c)

    for hd in range(N_HEADS):
        p, odd = hd // 2, hd % 2
        base = 0 if odd else 64
        ch, cm, cl = c_hi[:, hd:hd + 1], c_mid[:, hd:hd + 1], c_lo[:, hd:hd + 1]
        ones_q = (lane >= base + 3) & (lane < base + 6)
        eq = jnp.where(lane == base, ch, jnp.where(lane == base + 1, cm, jnp.where(lane == base + 2, cl,
                       jnp.where(ones_q, 1.0, 0.0))))
        ones_k = (lane >= base) & (lane < base + 3)
        ek = jnp.where(lane == base + 3, -ch, jnp.where(lane == base + 4, -cm, jnp.where(lane == base + 5, -cl,
                       jnp.where(ones_k, 1.0, 0.0))))
        val = (lane >= 64) if odd else (lane < 64)
        qa_ref[0, hd] = jnp.where(val, qn[:, p * LANES:(p + 1) * LANES], eq).astype(BF16)
        ka_ref[0, hd] = jnp.where(val, kn[:, p * LANES:(p + 1) * LANES], ek).astype(BF16)

    u = z[:, C_POOL:C_POOL + 256]
    uh_ref[POOL_HIST:POOL_HIST + tm, :] = u
    pos = t * tm + lax.broadcasted_iota(jnp.int32, (tm, LANES), 0)
    mixed = []
    for cb in range(2):
        wl = _pool_window_lanes((tm, LANES), cb)
        acc = u[:, cb * LANES:(cb + 1) * LANES]
        for j in range(1, POOL_WINDOWS[2 * cb + 1]):
            sh = uh_ref[POOL_HIST - j:POOL_HIST - j + tm, cb * LANES:(cb + 1) * LANES]
            acc = acc + jnp.where(wl > j, sh, 0.0)
        cnt = jnp.minimum(pos + 1, wl).astype(F32)
        mixed.append(acc / cnt - u[:, cb * LANES:(cb + 1) * LANES])
    mixed = jnp.concatenate(mixed, axis=1).astype(BF16)
    pm = jnp.dot(mixed, poolw_ref[...], preferred_element_type=F32) * pscale_ref[...]
    pm_ref[0] = pm.astype(BF16)
    ptail_ref[0] = uh_ref[tm:tm + POOL_HIST, :]
    uh_ref[0:POOL_HIST, :] = uh_ref[tm:tm + POOL_HIST, :]

    glu = z[:, C_CONV:C_CONV + 256] * jax.nn.sigmoid(z[:, C_CONV + 256:C_CONV + 512])
    gh_ref[CONV_HIST:CONV_HIST + tm, :] = glu
    y = jnp.zeros((tm, 256), F32) + dwb_ref[...]
    for j in range(CONV_WIDTH):
        off = CONV_HIST - CONV_BUF + j
        y = y + dww_ref[j:j + 1, :] * gh_ref[off:off + tm, :]
    ca_ref[0] = _layer_norm_silu(y, lng_ref[...], lnb_ref[...]).astype(BF16)
    ctail_ref[0] = gh_ref[tm:tm + CONV_HIST, :]
    gh_ref[0:CONV_HIST, :] = gh_ref[tm:tm + CONV_HIST, :]


def _inproj_prompt(x, lw, tm):
    B, T, D = x.shape
    nt = T // tm
    const = lambda shape: pl.BlockSpec(shape, lambda b, t: (0,) * len(shape))
    row = lambda w: pl.BlockSpec((1, tm, w), lambda b, t: (b, t, 0))
    heads = pl.BlockSpec((1, N_HEADS, tm, LANES), lambda b, t: (b, 0, t, 0))
    tail = lambda r: pl.BlockSpec((1, r, 256), lambda b, t: (b, 0, 0))
    out_shape = (
        jax.ShapeDtypeStruct((B, N_HEADS, T, LANES), BF16),
        jax.ShapeDtypeStruct((B, N_HEADS, T, LANES), BF16),
        jax.ShapeDtypeStruct((B, T, D_ATT), BF16),
        jax.ShapeDtypeStruct((B, T, D_ATT), F32),
        jax.ShapeDtypeStruct((B, T, D_ATT), F32),
        jax.ShapeDtypeStruct((B, T, N_HEADS), F32),
        jax.ShapeDtypeStruct((B, T, 256), BF16),
        jax.ShapeDtypeStruct((B, T, 256), BF16),
        jax.ShapeDtypeStruct((B, POOL_HIST, 256), F32),
        jax.ShapeDtypeStruct((B, CONV_HIST, 256), F32),
    )
    out_specs = (heads, heads, row(D_ATT), row(D_ATT), row(D_ATT), row(N_HEADS), row(256), row(256),
                 tail(POOL_HIST), tail(CONV_HIST))
    in_specs = [row(D), const((1, D)), const((D, D_MAIN)), const((1, LANES)), const((1, D_ATT)), const((1, D_ATT)),
                const((256, 256)), const((tm, tm)), const((256, 256)), const((1, 256)), const((CONV_HIST, 256)),
                const((1, 256)), const((1, 256)), const((1, 256))]
    return pl.pallas_call(
        functools.partial(_inproj_prompt_kernel, tm=tm),
        grid=(B, nt), in_specs=in_specs, out_specs=out_specs, out_shape=out_shape,
        scratch_shapes=[pltpu.VMEM((POOL_HIST + tm, 256), F32), pltpu.VMEM((CONV_HIST + tm, 256), F32),
                        pltpu.VMEM((1, LANES), F32)],
        compiler_params=_cparams(2), name="inproj_prompt",
    )(x, lw["n1"], lw["w_main"], lw["bf"], lw["qg"], lw["kg"], lw["gsum"], lw["ltri"], lw["poolw"],
      lw["pscale"], lw["dww"], lw["dwb"], lw["lng"], lw["lnb"])


def _attn_prompt_kernel(qa_ref, ka_ref, vb_ref, o_ref, m_ref, l_ref, acc_ref, *, tq):
    i = pl.program_id(2)
    row = lax.broadcasted_iota(jnp.int32, (tq, tq), 0)
    col = lax.broadcasted_iota(jnp.int32, (tq, tq), 1)
    causal = col <= row

    for hh in range(2):
        q = qa_ref[0, hh]
        m_ref[hh] = jnp.full((tq, 1), -jnp.inf, F32)
        l_ref[hh] = jnp.zeros((tq, 1), F32)
        acc_ref[hh] = jnp.zeros((tq, LANES), F32)

        def step(j, masked, hh=hh, q=q):
            start = pl.multiple_of(j * tq, tq)
            k = ka_ref[0, hh, pl.ds(start, tq), :]
            v = vb_ref[0, pl.ds(start, tq), :]
            s = lax.dot_general(q, k, (((1,), (1,)), ((), ())), preferred_element_type=F32)
            if masked:
                s = jnp.where(causal, s, -jnp.inf)
            m_old = m_ref[hh]
            m_new = jnp.maximum(m_old, jnp.max(s, axis=1, keepdims=True))
            p = jnp.exp(s - m_new)
            alpha = jnp.exp(m_old - m_new)
            l_ref[hh] = alpha * l_ref[hh] + jnp.sum(p, axis=1, keepdims=True)
            acc_ref[hh] = alpha * acc_ref[hh] + jnp.dot(p.astype(BF16), v, preferred_element_type=F32)
            m_ref[hh] = m_new

        def body(j, carry):
            step(j, False)
            return carry

        lax.fori_loop(0, i, body, 0)
        step(i, True)

    lane = lax.broadcasted_iota(jnp.int32, (tq, LANES), 1)
    o = jnp.where(lane < 64, acc_ref[0] / l_ref[0], acc_ref[1] / l_ref[1])
    o_ref[0] = o.astype(BF16)


def _attn_prompt(qa, ka, vb, tq):
    B, _, T, _ = qa.shape
    nq = T // tq
    return pl.pallas_call(
        functools.partial(_attn_prompt_kernel, tq=tq),
        grid=(B, N_PAIRS, nq),
        in_specs=[pl.BlockSpec((1, 2, tq, LANES), lambda b, p, i: (b, p, i, 0)),
                  pl.BlockSpec((1, 2, T, LANES), lambda b, p, i: (b, p, 0, 0)),
                  pl.BlockSpec((1, T, LANES), lambda b, p, i: (b, 0, p))],
        out_specs=pl.BlockSpec((1, tq, LANES), lambda b, p, i: (b, i, p)),
        out_shape=jax.ShapeDtypeStruct((B, T, D_ATT), BF16),
        scratch_shapes=[pltpu.VMEM((2, tq, 1), F32), pltpu.VMEM((2, tq, 1), F32), pltpu.VMEM((2, tq, LANES), F32)],
        compiler_params=_cparams(3), name="attn_prompt",
    )(qa, ka, vb)


def _merge_kernel(x_ref, att_ref, pm_ref, ca_ref, n1_ref, wg_ref, bg_ref, wau_ref, wpu_ref, wco_ref, wo_ref, o_ref):
    x = x_ref[...]
    d = x.shape[1]
    h = _rms(x, n1_ref[...]).astype(BF16)
    branches = (jnp.dot(att_ref[...], wau_ref[...], preferred_element_type=F32),
                jnp.dot(pm_ref[...], wpu_ref[...], preferred_element_type=F32),
                jnp.dot(ca_ref[...], wco_ref[...], preferred_element_type=F32))
    merged = jnp.zeros_like(x)
    for br in range(3):
        gl = jnp.dot(h, wg_ref[:, br * d:(br + 1) * d], preferred_element_type=F32) + bg_ref[br:br + 1, :]
        merged = merged + jax.nn.sigmoid(gl) * branches[br]
    o_ref[...] = x + jnp.dot(merged.astype(BF16), wo_ref[...], preferred_element_type=F32)


def _merge(x, att, pm, ca, lw, tm):
    n, d = x.shape
    const = lambda shape: pl.BlockSpec(shape, lambda t: (0, 0))
    row = lambda w: pl.BlockSpec((tm, w), lambda t: (t, 0))
    return pl.pallas_call(
        _merge_kernel, grid=(n // tm,),
        in_specs=[row(d), row(D_ATT), row(256), row(256), const((1, d)), const((d, 3 * d)), const((3, d)),
                  const((D_ATT, d)), const((256, d)), const((256, d)), const((d, d))],
        out_specs=row(d), out_shape=jax.ShapeDtypeStruct((n, d), F32),
        compiler_params=_cparams(1), name="merge",
    )(x, att, pm, ca, lw["n1"], lw["w_gate"], lw["bg"], lw["w_att_up"], lw["w_pool_up"], lw["w_conv_out"], lw["w_out"])


def _mlp_kernel(x_ref, n2_ref, wu_ref, wd_ref, o_ref):
    x = x_ref[...]
    h = _rms(x, n2_ref[...]).astype(BF16)
    a = jnp.maximum(jnp.dot(h, wu_ref[...], preferred_element_type=F32), 0.0)
    hid = (a * a).astype(BF16)
    o_ref[...] = x + jnp.dot(hid, wd_ref[...], preferred_element_type=F32)


def _mlp(x, lw, tm):
    n, d = x.shape
    dff = lw["w_up"].shape[1]
    single = pl.Buffered(1)
    return pl.pallas_call(
        _mlp_kernel, grid=(n // tm,),
        in_specs=[pl.BlockSpec((tm, d), lambda t: (t, 0)), pl.BlockSpec((1, d), lambda t: (0, 0)),
                  pl.BlockSpec((d, dff), lambda t: (0, 0), pipeline_mode=single),
                  pl.BlockSpec((dff, d), lambda t: (0, 0), pipeline_mode=single)],
        out_specs=pl.BlockSpec((tm, d), lambda t: (t, 0)), out_shape=jax.ShapeDtypeStruct((n, d), F32),
        compiler_params=_cparams(1), name="mlp",
    )(x, lw["n2"], lw["w_up"], lw["w_down"])


def _inproj_sample_kernel(x_ref, n1_ref, w_ref, bf_ref, qg_ref, kg_ref, gsum_ref, poolw_ref, pscale_ref,
                          dww_ref, dwb_ref, lng_ref, lnb_ref, sp_ref, sc_ref,
                          q_ref, k_ref, v_ref, logf_ref, u_ref, glu_ref, pm_ref, ca_ref, *, start_pos):
    x = x_ref[...]
    n = x.shape[0]
    h = _rms(x, n1_ref[...]).astype(BF16)
    z = jnp.dot(h, w_ref[...], preferred_element_type=F32)
    gsum = gsum_ref[...]
    q_ref[...] = (_head_norm(z[:, C_Q:C_Q + D_ATT], gsum, qg_ref[...]) * (HEAD_DIM ** -0.5)).astype(BF16)
    k_ref[...] = _head_norm(z[:, C_K:C_K + D_ATT], gsum, kg_ref[...])
    v_ref[...] = z[:, C_V:C_V + D_ATT]
    logf_ref[...] = _log_sigmoid(z[:, C_F:C_F + LANES] + bf_ref[...])

    u = z[:, C_POOL:C_POOL + 256]
    u_ref[...] = u
    mixed = []
    for cb in range(2):
        wl = _pool_window_lanes((n, LANES), cb)
        acc = u[:, cb * LANES:(cb + 1) * LANES]
        for j in range(1, POOL_WINDOWS[2 * cb + 1]):
            acc = acc + jnp.where(wl > j, sp_ref[POOL_BUF - j, :, cb * LANES:(cb + 1) * LANES], 0.0)
        cnt = jnp.minimum(start_pos + 1, wl).astype(F32)
        mixed.append(acc / cnt - u[:, cb * LANES:(cb + 1) * LANES])
    mixed = jnp.concatenate(mixed, axis=1).astype(BF16)
    pm_ref[...] = (jnp.dot(mixed, poolw_ref[...], preferred_element_type=F32) * pscale_ref[...]).astype(BF16)

    glu = z[:, C_CONV:C_CONV + 256] * jax.nn.sigmoid(z[:, C_CONV + 256:C_CONV + 512])
    glu_ref[...] = glu
    y = dwb_ref[...] + dww_ref[CONV_BUF:CONV_BUF + 1, :] * glu
    for j in range(CONV_BUF):
        y = y + dww_ref[j:j + 1, :] * sc_ref[j]
    ca_ref[...] = _layer_norm_silu(y, lng_ref[...], lnb_ref[...]).astype(BF16)


def _inproj_sample(x, lw, sp_t, sc_t, start_pos):
    n, d = x.shape
    sds = jax.ShapeDtypeStruct
    out_shape = (sds((n, D_ATT), BF16), sds((n, D_ATT), F32), sds((n, D_ATT), F32), sds((n, LANES), F32),
                 sds((n, 256), F32), sds((n, 256), F32), sds((n, 256), BF16), sds((n, 256), BF16))
    return pl.pallas_call(
        functools.partial(_inproj_sample_kernel, start_pos=start_pos),
        out_shape=out_shape,
        compiler_params=pltpu.CompilerParams(vmem_limit_bytes=VMEM_LIMIT), name="inproj_sample",
    )(x, lw["n1"], lw["w_main"], lw["bf"], lw["qg"], lw["kg"], lw["gsum"], lw["poolw"], lw["pscale"],
      lw["dww"], lw["dwb"], lw["lng"], lw["lnb"], sp_t, sc_t)


def _page_cumsum_kernel(f_ref, mc_ref, o_ref):
    hi, mid, lo = _split3(f_ref[...])
    mc = mc_ref[...]
    o_ref[...] = (jnp.dot(hi.astype(BF16), mc, preferred_element_type=F32)
                  + jnp.dot(mid.astype(BF16), mc, preferred_element_type=F32)
                  + jnp.dot(lo.astype(BF16), mc, preferred_element_type=F32))


def _page_cumsum(logf_pages, layer, mc, tp):
    _, npg, w = logf_pages.shape
    return pl.pallas_call(
        _page_cumsum_kernel, grid=(npg // tp,),
        in_specs=[pl.BlockSpec((None, tp, w), lambda i: (layer, i, 0)), pl.BlockSpec((w, w), lambda i: (0, 0))],
        out_specs=pl.BlockSpec((tp, w), lambda i: (i, 0)), out_shape=jax.ShapeDtypeStruct((npg, w), F32),
        compiler_params=_cparams(1), name="page_cumsum",
    )(logf_pages, mc)


def _attn_sample_kernel(pt_ref, q_ref, kn_ref, vn_ref, fn_ref, *refs, pp, page):
    k_refs, v_refs, c_refs = refs[:pp], refs[pp:2 * pp], refs[2 * pp:3 * pp]
    o_ref = refs[3 * pp]
    m_ref, l_ref, acc_ref, carry_ref = refs[3 * pp + 1:]
    j = pl.program_id(1)
    nj = pl.num_programs(1)

    @pl.when(j == 0)
    def _():
        m_ref[...] = jnp.full((N_HEADS, 1), -jnp.inf, F32)
        l_ref[...] = jnp.zeros((N_HEADS, 1), F32)
        acc_ref[...] = jnp.zeros((N_HEADS, D_ATT), F32)
        carry_ref[...] = jnp.zeros((N_HEADS, 1), F32)

    head_of_lane = lax.broadcasted_iota(jnp.int32, (N_HEADS, D_ATT), 1) // HEAD_DIM
    own = head_of_lane == lax.broadcasted_iota(jnp.int32, (N_HEADS, D_ATT), 0)
    qbd32 = jnp.where(own, jnp.broadcast_to(q_ref[0].astype(F32), (N_HEADS, D_ATT)), 0.0)
    qbd = qbd32.astype(BF16)

    s_list = []
    carry = carry_ref[...]
    for i in range(pp):
        kt = k_refs[i][...].astype(BF16)
        s = jnp.dot(qbd, kt, preferred_element_type=F32)
        cs = c_refs[i][...]
        s_list.append(s - (cs + carry))
        carry = carry + cs[:, page - 1:page]
    carry_ref[...] = carry

    m_old = m_ref[...]
    m_blk = s_list[0]
    for s in s_list[1:]:
        m_blk = jnp.maximum(m_blk, s)
    m_new = jnp.maximum(m_old, jnp.max(m_blk, axis=1, keepdims=True))
    alpha = jnp.exp(m_old - m_new)
    p_sum = jnp.zeros((N_HEADS, page), F32)
    acc = alpha * acc_ref[...]
    for i in range(pp):
        p = jnp.exp(s_list[i] - m_new)
        p_sum = p_sum + p
        acc = acc + lax.dot_general(p.astype(BF16), v_refs[i][...].astype(BF16), (((1,), (1,)), ((), ())),
                                    preferred_element_type=F32)
    l_new = alpha * l_ref[...] + jnp.sum(p_sum, axis=1, keepdims=True)
    m_ref[...] = m_new
    l_ref[...] = l_new
    acc_ref[...] = acc

    @pl.when(j == nj - 1)
    def _():
        kn = jnp.broadcast_to(kn_ref[0], (N_HEADS, D_ATT)).astype(BF16).astype(F32)
        vn = jnp.broadcast_to(vn_ref[0], (N_HEADS, D_ATT)).astype(BF16).astype(F32)
        fn = fn_ref[0]
        s_new = jnp.sum(qbd32 * kn, axis=1, keepdims=True) - (carry + fn)
        m_fin = jnp.maximum(m_new, s_new)
        a_fin = jnp.exp(m_new - m_fin)
        p_new = jnp.exp(s_new - m_fin)
        l_fin = a_fin * l_new + p_new
        acc_fin = a_fin * acc + p_new.astype(BF16).astype(F32) * vn
        o = jnp.where(own, acc_fin / l_fin, 0.0)
        o_ref[0] = jnp.sum(o, axis=0, keepdims=True).astype(BF16)


def _attn_sample(page_table, q, k_new, v_new, logf_new, cache_k, cache_v, layer, cs_l, pp):
    bd, n_pages = page_table.shape
    page = cache_k.shape[3]
    nj = n_pages // pp
    tok = lambda w: pl.BlockSpec((1, 1, w), lambda b, j, pt: (b, 0, 0))
    kv_spec = lambda i: pl.BlockSpec((None, None, D_ATT, page),
                                     lambda b, j, pt, i=i: (layer, pt[b, j * pp + i], 0, 0))
    c_spec = lambda i: pl.BlockSpec((None, N_HEADS, page), lambda b, j, pt, i=i: (pt[b, j * pp + i], 0, 0))
    in_specs = ([tok(D_ATT), tok(D_ATT), tok(D_ATT), pl.BlockSpec((1, N_HEADS, 1), lambda b, j, pt: (b, 0, 0))]
                + [kv_spec(i) for i in range(pp)] + [kv_spec(i) for i in range(pp)] + [c_spec(i) for i in range(pp)])
    grid_spec = pltpu.PrefetchScalarGridSpec(
        num_scalar_prefetch=1, grid=(bd, nj), in_specs=in_specs,
        out_specs=pl.BlockSpec((1, 1, D_ATT), lambda b, j, pt: (b, 0, 0)),
        scratch_shapes=[pltpu.VMEM((N_HEADS, 1), F32), pltpu.VMEM((N_HEADS, 1), F32),
                        pltpu.VMEM((N_HEADS, D_ATT), F32), pltpu.VMEM((N_HEADS, 1), F32)])
    out = pl.pallas_call(
        functools.partial(_attn_sample_kernel, pp=pp, page=page),
        grid_spec=grid_spec, out_shape=jax.ShapeDtypeStruct((bd, 1, D_ATT), BF16),
        compiler_params=_cparams(2), name="attn_sample",
    )(page_table, q[:, None, :], k_new[:, None, :], v_new[:, None, :], logf_new[:, :N_HEADS, None],
      *([cache_k] * pp), *([cache_v] * pp), *([cs_l] * pp))
    return out[:, 0, :]


def _layer_weights(l, tm, norm1_g, w_in, b_f, b_gate, q_gain, k_gain, pool_w, pool_scale, w_pool_up, w_att_up,
                   dw_w, dw_b, conv_ln_g, conv_ln_b, w_conv_out, w_out, norm2_g, w_mlp_up, w_mlp_down):
    d = w_in.shape[1]
    w = w_in[l]
    o_f = 3 * D_ATT
    o_pool = o_f + N_HEADS
    o_conv = o_pool + 256
    o_gate = o_conv + 512
    w_main = jnp.concatenate([w[:, :o_f], w[:, o_pool:o_gate], w[:, o_f:o_pool],
                              jnp.zeros((d, LANES - N_HEADS), F32)], axis=1).astype(BF16)
    grp = jnp.arange(256) // HEAD_DIM
    poolw = jnp.zeros((256, 256), F32)
    for g in range(4):
        poolw = poolw.at[g * 64:(g + 1) * 64, g * 64:(g + 1) * 64].set(pool_w[l, g])
    return dict(
        n1=norm1_g[l][None, :], w_main=w_main, w_gate=w[:, o_gate:].astype(BF16),
        bf=jnp.pad(b_f[l], (0, LANES - N_HEADS))[None, :], bg=b_gate[l],
        qg=jnp.tile(q_gain[l], N_HEADS)[None, :], kg=jnp.tile(k_gain[l], N_HEADS)[None, :],
        gsum=(grp[:, None] == grp[None, :]).astype(BF16),
        ltri=(jnp.arange(tm)[:, None] >= jnp.arange(tm)[None, :]).astype(BF16),
        poolw=poolw.astype(BF16), pscale=pool_scale[l][None, :],
        dww=jnp.pad(dw_w[l], ((0, CONV_HIST - CONV_WIDTH), (0, 0))), dwb=dw_b[l][None, :],
        lng=conv_ln_g[l][None, :], lnb=conv_ln_b[l][None, :],
        w_att_up=w_att_up[l].astype(BF16), w_pool_up=w_pool_up[l].astype(BF16),
        w_conv_out=w_conv_out[l].astype(BF16), w_out=w_out[l].astype(BF16),
        n2=norm2_g[l][None, :], w_up=w_mlp_up[l].astype(BF16), w_down=w_mlp_down[l].astype(BF16))


def kernel(x_prompt, x_sample, cache_k, cache_v, cache_logf, state_pool, state_conv, page_table,
           norm1_g, w_in, b_f, b_gate, q_gain, k_gain, pool_w, pool_scale, w_pool_up, w_att_up,
           dw_w, dw_b, conv_ln_g, conv_ln_b, w_conv_out, w_out, norm2_g, w_mlp_up, w_mlp_down):
    B, T, D = x_prompt.shape
    bd = x_sample.shape[0]
    depth = w_in.shape[0]
    n_pool_pages, page = cache_k.shape[1], cache_k.shape[2]
    n_pages = page_table.shape[1]
    past_len = n_pages * page
    tm = min(512, T)
    pp = min(16, n_pages)

    mc = (jnp.arange(page)[:, None] <= jnp.arange(page)[None, :]).astype(BF16)
    n_rows = n_pool_pages * N_HEADS
    tp = 4096 if n_rows % 4096 == 0 else n_rows
    ck_flat = jnp.transpose(cache_k, (0, 1, 3, 4, 2)).reshape(depth, n_pool_pages, D_ATT, page)
    cv_flat = jnp.transpose(cache_v, (0, 1, 3, 4, 2)).reshape(depth, n_pool_pages, D_ATT, page)
    logf_flat = jnp.transpose(cache_logf, (0, 1, 3, 2)).reshape(depth, n_rows, page)

    xp = x_prompt
    xs = x_sample.reshape(bd, D)
    outs = [[] for _ in range(10)]
    for l in range(depth):
        lw = _layer_weights(l, tm, norm1_g, w_in, b_f, b_gate, q_gain, k_gain, pool_w, pool_scale, w_pool_up,
                            w_att_up, dw_w, dw_b, conv_ln_g, conv_ln_b, w_conv_out, w_out, norm2_g,
                            w_mlp_up, w_mlp_down)
        qa, ka, vb, kp, vp, fp, pm, ca, ptail, ctail = _inproj_prompt(xp, lw, tm)
        att = _attn_prompt(qa, ka, vb, tm)
        x1 = _merge(xp.reshape(B * T, D), att.reshape(B * T, D_ATT), pm.reshape(B * T, 256),
                    ca.reshape(B * T, 256), lw, tm)
        xp = _mlp(x1, lw, tm).reshape(B, T, D)
        sp_t = jnp.transpose(state_pool[l], (1, 0, 2))
        sc_t = jnp.transpose(state_conv[l], (1, 0, 2))
        qs, ks, vs, fs, us, gs, pms, cas = _inproj_sample(xs, lw, sp_t, sc_t, past_len)
        cs_l = _page_cumsum(logf_flat, l, mc, tp)
        att_s = _attn_sample(page_table, qs, ks, vs, fs, ck_flat, cv_flat, l,
                             cs_l.reshape(n_pool_pages, N_HEADS, page), pp)
        xs1 = _merge(xs, att_s, pms, cas, lw, bd)
        xs = _mlp(xs1, lw, bd)

        outs[0].append(kp.reshape(B, T, N_HEADS, HEAD_DIM))
        outs[1].append(vp.reshape(B, T, N_HEADS, HEAD_DIM))
        outs[2].append(fp)
        outs[3].append(ptail[:, POOL_HIST - POOL_BUF:])
        outs[4].append(ctail[:, CONV_HIST - CONV_BUF:])
        outs[5].append(ks.reshape(bd, 1, N_HEADS, HEAD_DIM))
        outs[6].append(vs.reshape(bd, 1, N_HEADS, HEAD_DIM))
        outs[7].append(fs[:, None, :N_HEADS])
        outs[8].append(jnp.concatenate([state_pool[l][:, 1:], us[:, None, :]], axis=1))
        outs[9].append(jnp.concatenate([state_conv[l][:, 1:], gs[:, None, :]], axis=1))
    return (xp, xs.reshape(bd, 1, D)) + tuple(jnp.stack(o) for o in outs)
```

```python
import functools

import jax
import jax.numpy as jnp
from jax import lax
from jax.experimental import pallas as pl
from jax.experimental.pallas import tpu as pltpu

F32 = jnp.float32
BF16 = jnp.bfloat16

N_HEADS = 8
HEAD_DIM = 64
D_ATT = N_HEADS * HEAD_DIM
N_PAIRS = N_HEADS // 2
POOL_WINDOWS = (2, 4, 8, 16)
POOL_BUF = max(POOL_WINDOWS) - 1
POOL_HIST = 16
CONV_WIDTH = 31
CONV_BUF = CONV_WIDTH - 1
CONV_HIST = 32
NORM_EPS = 1e-6
LN_EPS = 1e-5
LOG2E = 1.4426950408889634
Q_SPLIT = 2
LANES = 128
VMEM_LIMIT = 56 * 1024 * 1024

C_Q, C_K, C_V, C_POOL, C_CONV, C_F = 0, 512, 1024, 1536, 1792, 2304
D_MAIN = C_F + LANES


def _cparams(n_axes):
    return pltpu.CompilerParams(dimension_semantics=("arbitrary",) * n_axes, vmem_limit_bytes=VMEM_LIMIT)


def _split3(x):
    hi = x.astype(BF16).astype(F32)
    r = x - hi
    mid = r.astype(BF16).astype(F32)
    lo = (r - mid).astype(BF16).astype(F32)
    return hi, mid, lo


def _rms(x, g):
    return x * lax.rsqrt(jnp.mean(x * x, axis=-1, keepdims=True) + NORM_EPS) * g


def _log_sigmoid(x):
    return jnp.minimum(x, 0.0) - jnp.log1p(jnp.exp(-jnp.abs(x)))


def _head_norm(a, gsum, gain):
    outs = []
    for c in range(2):
        blk = a[:, c * 256:(c + 1) * 256]
        sq = blk * blk
        hi = sq.astype(BF16)
        lo = (sq - hi.astype(F32)).astype(BF16)
        ss = jnp.dot(hi, gsum, preferred_element_type=F32) + jnp.dot(lo, gsum, preferred_element_type=F32)
        outs.append(blk * lax.rsqrt(ss * (1.0 / HEAD_DIM) + NORM_EPS))
    return jnp.concatenate(outs, axis=1) * gain


def _pool_window_lanes(shape, col_block):
    lane = lax.broadcasted_iota(jnp.int32, shape, len(shape) - 1)
    lo_w, hi_w = POOL_WINDOWS[2 * col_block], POOL_WINDOWS[2 * col_block + 1]
    return jnp.where(lane < 64, lo_w, hi_w)


def _layer_norm_silu(y, g, b):
    mu = jnp.mean(y, axis=-1, keepdims=True)
    d = y - mu
    var = jnp.mean(d * d, axis=-1, keepdims=True)
    yn = d * lax.rsqrt(var + LN_EPS) * g + b
    return yn * jax.nn.sigmoid(yn)


def _inproj_prompt_kernel(x_ref, n1_ref, w_ref, bf_ref, qg_ref, kg_ref, gsum_ref, ltri_ref, poolw_ref,
                          pscale_ref, dww_ref, dwb_ref, lng_ref, lnb_ref,
                          *rest, tm, n_prev):
    (qt_ref, ka_ref, vtb_ref, kt_ref, vt_ref, logft_ref, pm_ref, ca_ref, ptail_ref, ctail_ref,
     uh_ref, gh_ref, carry_ref) = rest[n_prev:]
    t = pl.program_id(1)

    @pl.when(t == 0)
    def _():
        uh_ref[0:POOL_HIST, :] = jnp.zeros((POOL_HIST, 256), F32)
        gh_ref[0:CONV_HIST, :] = jnp.zeros((CONV_HIST, 256), F32)
        carry_ref[...] = jnp.zeros((1, LANES), F32)

    x = x_ref[0]
    h = _rms(x, n1_ref[...]).astype(BF16)
    z = jnp.dot(h, w_ref[...], preferred_element_type=F32)

    gsum = gsum_ref[...]
    qn = _head_norm(z[:, C_Q:C_Q + D_ATT], gsum, qg_ref[...]) * (HEAD_DIM ** -0.5 * LOG2E)
    kn = _head_norm(z[:, C_K:C_K + D_ATT], gsum, kg_ref[...])
    qt = qn.T
    vt = z[:, C_V:C_V + D_ATT].T
    kt_ref[0] = kn.T
    vt_ref[0] = vt

    lane = lax.broadcasted_iota(jnp.int32, (tm, LANES), 1)
    logf = jnp.where(lane < N_HEADS, _log_sigmoid(z[:, C_F:C_F + LANES] + bf_ref[...]), 0.0)
    logft_ref[0] = logf.T[0:N_HEADS, :]
    f_hi, f_mid, f_lo = _split3(logf)
    f3 = (f_hi + pltpu.roll(f_mid, 8, 1) + pltpu.roll(f_lo, 16, 1)).astype(BF16)
    cs3 = jnp.dot(ltri_ref[...], f3, preferred_element_type=F32)
    c = cs3 + pltpu.roll(cs3, LANES - 8, 1) + pltpu.roll(cs3, LANES - 16, 1) + carry_ref[...]
    c = jnp.where(lane < N_HEADS, c, 0.0)
    carry_ref[...] = c[tm - 1:tm, :]
    c2 = c * LOG2E
    c_hi, c_mid, c_lo = _split3(c2)
    ct_hi, ct_mid, ct_lo = _split3(c2.T[0:N_HEADS, :])

    row8 = lax.broadcasted_iota(jnp.int32, (8, tm), 0)
    pad_rows = jnp.zeros((HEAD_DIM - 8, tm), F32)
    for hd in range(N_HEADS):
        p, odd = hd // 2, hd % 2
        base = 0 if odd else 64
        ch, cm, cl = c_hi[:, hd:hd + 1], c_mid[:, hd:hd + 1], c_lo[:, hd:hd + 1]
        ones_k = (lane >= base) & (lane < base + 3)
        ek = jnp.where(lane == base + 3, -ch, jnp.where(lane == base + 4, -cm, jnp.where(lane == base + 5, -cl,
                       jnp.where(ones_k, 1.0, 0.0))))
        val = (lane >= 64) if odd else (lane < 64)
        ka_ref[0, hd] = jnp.where(val, kn[:, p * LANES:(p + 1) * LANES], ek).astype(BF16)
        eq = jnp.where(row8 == 0, ct_hi[hd:hd + 1, :], jnp.where(row8 == 1, ct_mid[hd:hd + 1, :],
                       jnp.where(row8 == 2, ct_lo[hd:hd + 1, :], jnp.where(row8 < 6, 1.0, 0.0))))
        qh = qt[hd * HEAD_DIM:(hd + 1) * HEAD_DIM, :]
        parts = [eq, pad_rows, qh] if odd else [qh, eq, pad_rows]
        qt_ref[0, hd] = jnp.concatenate(parts, axis=0).astype(BF16)
        vtb_ref[0, hd, 0] = vt[hd * HEAD_DIM:(hd + 1) * HEAD_DIM, :].astype(BF16)

    u = z[:, C_POOL:C_POOL + 256]
    uh_ref[POOL_HIST:POOL_HIST + tm, :] = u
    pos = t * tm + lax.broadcasted_iota(jnp.int32, (tm, LANES), 0)
    mixed = []
    for cb in range(2):
        wl = _pool_window_lanes((tm, LANES), cb)
        acc = u[:, cb * LANES:(cb + 1) * LANES]
        for j in range(1, POOL_WINDOWS[2 * cb + 1]):
            sh = uh_ref[POOL_HIST - j:POOL_HIST - j + tm, cb * LANES:(cb + 1) * LANES]
            acc = acc + jnp.where(wl > j, sh, 0.0)
        cnt = jnp.minimum(pos + 1, wl).astype(F32)
        mixed.append(acc / cnt - u[:, cb * LANES:(cb + 1) * LANES])
    mixed = jnp.concatenate(mixed, axis=1).astype(BF16)
    pm = jnp.dot(mixed, poolw_ref[...], preferred_element_type=F32) * pscale_ref[...]
    pm_ref[0] = pm.astype(BF16)
    ptail_ref[0] = uh_ref[tm:tm + POOL_HIST, :]
    uh_ref[0:POOL_HIST, :] = uh_ref[tm:tm + POOL_HIST, :]

    glu = z[:, C_CONV:C_CONV + 256] * jax.nn.sigmoid(z[:, C_CONV + 256:C_CONV + 512])
    gh_ref[CONV_HIST:CONV_HIST + tm, :] = glu
    y = jnp.zeros((tm, 256), F32) + dwb_ref[...]
    for j in range(CONV_WIDTH):
        off = CONV_HIST - CONV_BUF + j
        y = y + dww_ref[j:j + 1, :] * gh_ref[off:off + tm, :]
    ca_ref[0] = _layer_norm_silu(y, lng_ref[...], lnb_ref[...]).astype(BF16)
    ctail_ref[0] = gh_ref[tm:tm + CONV_HIST, :]
    gh_ref[0:CONV_HIST, :] = gh_ref[tm:tm + CONV_HIST, :]


def _inproj_prompt(x, lw, tm, layer, depth, prev):
    B, T, D = x.shape
    nt = T // tm
    const = lambda shape: pl.BlockSpec(shape, lambda b, t: (0,) * len(shape))
    row = lambda w: pl.BlockSpec((1, tm, w), lambda b, t: (b, t, 0))
    tail = lambda r: pl.BlockSpec((1, r, 256), lambda b, t: (b, 0, 0))
    stack = lambda r: pl.BlockSpec((None, 1, r, tm), lambda b, t: (layer, b, 0, t))
    out_shape = (
        jax.ShapeDtypeStruct((B, N_HEADS, LANES, T), BF16),
        jax.ShapeDtypeStruct((B, N_HEADS, T, LANES), BF16),
        jax.ShapeDtypeStruct((B, N_HEADS, nt, HEAD_DIM, tm), BF16),
        jax.ShapeDtypeStruct((depth, B, D_ATT, T), F32),
        jax.ShapeDtypeStruct((depth, B, D_ATT, T), F32),
        jax.ShapeDtypeStruct((depth, B, N_HEADS, T), F32),
        jax.ShapeDtypeStruct((B, T, 256), BF16),
        jax.ShapeDtypeStruct((B, T, 256), BF16),
        jax.ShapeDtypeStruct((B, POOL_HIST, 256), F32),
        jax.ShapeDtypeStruct((B, CONV_HIST, 256), F32),
    )
    out_specs = (pl.BlockSpec((1, N_HEADS, LANES, tm), lambda b, t: (b, 0, 0, t)),
                 pl.BlockSpec((1, N_HEADS, tm, LANES), lambda b, t: (b, 0, t, 0)),
                 pl.BlockSpec((1, N_HEADS, 1, HEAD_DIM, tm), lambda b, t: (b, 0, t, 0, 0)),
                 stack(D_ATT), stack(D_ATT), stack(N_HEADS), row(256), row(256), tail(POOL_HIST), tail(CONV_HIST))
    in_specs = [row(D), const((1, D)), const((D, D_MAIN)), const((1, LANES)), const((1, D_ATT)), const((1, D_ATT)),
                const((256, 256)), const((tm, tm)), const((256, 256)), const((1, 256)), const((CONV_HIST, 256)),
                const((1, 256)), const((1, 256)), const((1, 256))]
    n_in = len(in_specs)
    in_specs += [pl.BlockSpec(memory_space=pl.ANY)] * len(prev)
    aliases = {n_in + i: 3 + i for i in range(len(prev))}
    return pl.pallas_call(
        functools.partial(_inproj_prompt_kernel, tm=tm, n_prev=len(prev)),
        grid=(B, nt), in_specs=in_specs, out_specs=out_specs, out_shape=out_shape,
        scratch_shapes=[pltpu.VMEM((POOL_HIST + tm, 256), F32), pltpu.VMEM((CONV_HIST + tm, 256), F32),
                        pltpu.VMEM((1, LANES), F32)],
        input_output_aliases=aliases,
        compiler_params=_cparams(2), name="inproj_prompt",
    )(x, lw["n1"], lw["w_main"], lw["bf"], lw["qg"], lw["kg"], lw["gsum"], lw["ltri"], lw["poolw"],
      lw["pscale"], lw["dww"], lw["dwb"], lw["lng"], lw["lnb"], *prev)


def _attn_prompt_kernel(qt_ref, ka_ref, vt_ref, o_ref, m_ref, l_ref, acc_ref, *, tq):
    i = pl.program_id(2)
    for hh in range(2):
        m_ref[hh] = jnp.full((1, tq), -jnp.inf, F32)
        l_ref[hh] = jnp.zeros((1, tq), F32)
        acc_ref[hh] = jnp.zeros((HEAD_DIM, tq), F32)

    tqs = tq // Q_SPLIT
    chains = [(hh, qs) for hh in range(2) for qs in range(Q_SPLIT)]

    def block(j, masked):
        start = pl.multiple_of(j * tq, tq)
        sts = [jnp.dot(ka_ref[0, hh, pl.ds(start, tq), :], qt_ref[0, hh, :, qs * tqs:(qs + 1) * tqs],
                       preferred_element_type=F32) for hh, qs in chains]
        for (hh, qs), st in zip(chains, sts):
            ql = slice(qs * tqs, (qs + 1) * tqs)
            if masked:
                key = lax.broadcasted_iota(jnp.int32, (tq, tqs), 0)
                qry = lax.broadcasted_iota(jnp.int32, (tq, tqs), 1) + qs * tqs
                st = jnp.where(key <= qry, st, -jnp.inf)
            m_old = m_ref[hh, :, ql]
            m_new = jnp.maximum(m_old, jnp.max(st, axis=0, keepdims=True))
            p = jnp.exp2(st - m_new)
            alpha = jnp.exp2(m_old - m_new)
            l_ref[hh, :, ql] = alpha * l_ref[hh, :, ql] + jnp.sum(p, axis=0, keepdims=True)
            acc_ref[hh, :, ql] = alpha * acc_ref[hh, :, ql] + jnp.dot(
                vt_ref[0, hh, j], p.astype(BF16), preferred_element_type=F32)
            m_ref[hh, :, ql] = m_new

    def body(j, carry):
        block(j, False)
        return carry

    lax.fori_loop(0, i, body, 0)
    block(i, True)
    o = jnp.concatenate([acc_ref[0] / l_ref[0], acc_ref[1] / l_ref[1]], axis=0)
    o_ref[0] = o.astype(BF16)


def _attn_prompt(qt, ka, vtb, tq):
    B, _, _, T = qt.shape
    nq = T // tq
    return pl.pallas_call(
        functools.partial(_attn_prompt_kernel, tq=tq),
        grid=(B, N_PAIRS, nq),
        in_specs=[pl.BlockSpec((1, 2, LANES, tq), lambda b, p, i: (b, p, 0, i)),
                  pl.BlockSpec((1, 2, T, LANES), lambda b, p, i: (b, p, 0, 0)),
                  pl.BlockSpec((1, 2, nq, HEAD_DIM, tq), lambda b, p, i: (b, p, 0, 0, 0))],
        out_specs=pl.BlockSpec((1, 2 * HEAD_DIM, tq), lambda b, p, i: (b, p, i)),
        out_shape=jax.ShapeDtypeStruct((B, D_ATT, T), BF16),
        scratch_shapes=[pltpu.VMEM((2, 1, tq), F32), pltpu.VMEM((2, 1, tq), F32),
                        pltpu.VMEM((2, HEAD_DIM, tq), F32)],
        compiler_params=_cparams(3), name="attn_prompt",
    )(qt, ka, vtb)


def _merge_kernel(x_ref, att_ref, pm_ref, ca_ref, n1_ref, wg_ref, bg_ref, wau_ref, wpu_ref, wco_ref, wo_ref, o_ref):
    x = x_ref[0]
    d = x.shape[1]
    h = _rms(x, n1_ref[...]).astype(BF16)
    branches = (lax.dot_general(att_ref[0], wau_ref[...], (((0,), (0,)), ((), ())), preferred_element_type=F32),
                jnp.dot(pm_ref[0], wpu_ref[...], preferred_element_type=F32),
                jnp.dot(ca_ref[0], wco_ref[...], preferred_element_type=F32))
    merged = jnp.zeros_like(x)
    for br in range(3):
        gl = jnp.dot(h, wg_ref[:, br * d:(br + 1) * d], preferred_element_type=F32) + bg_ref[br:br + 1, :]
        merged = merged + jax.nn.sigmoid(gl) * branches[br]
    o_ref[0] = x + jnp.dot(merged.astype(BF16), wo_ref[...], preferred_element_type=F32)


def _merge(x, att_t, pm, ca, lw, tm):
    B, T, d = x.shape
    const = lambda shape: pl.BlockSpec(shape, lambda b, t: (0, 0))
    row = lambda w: pl.BlockSpec((1, tm, w), lambda b, t: (b, t, 0))
    return pl.pallas_call(
        _merge_kernel, grid=(B, T // tm),
        in_specs=[row(d), pl.BlockSpec((1, D_ATT, tm), lambda b, t: (b, 0, t)), row(256), row(256),
                  const((1, d)), const((d, 3 * d)), const((3, d)),
                  const((D_ATT, d)), const((256, d)), const((256, d)), const((d, d))],
        out_specs=row(d), out_shape=jax.ShapeDtypeStruct((B, T, d), F32),
        compiler_params=_cparams(2), name="merge",
    )(x, att_t, pm, ca, lw["n1"], lw["w_gate"], lw["bg"], lw["w_att_up"], lw["w_pool_up"], lw["w_conv_out"], lw["w_out"])


def _mlp_kernel(x_ref, n2_ref, wu_ref, wd_ref, o_ref):
    x = x_ref[...]
    h = _rms(x, n2_ref[...]).astype(BF16)
    a = jnp.maximum(jnp.dot(h, wu_ref[...], preferred_element_type=F32), 0.0)
    hid = (a * a).astype(BF16)
    o_ref[...] = x + jnp.dot(hid, wd_ref[...], preferred_element_type=F32)


def _mlp(x, lw, tm):
    n, d = x.shape
    dff = lw["w_up"].shape[1]
    single = pl.Buffered(1)
    return pl.pallas_call(
        _mlp_kernel, grid=(n // tm,),
        in_specs=[pl.BlockSpec((tm, d), lambda t: (t, 0)), pl.BlockSpec((1, d), lambda t: (0, 0)),
                  pl.BlockSpec((d, dff), lambda t: (0, 0), pipeline_mode=single),
                  pl.BlockSpec((dff, d), lambda t: (0, 0), pipeline_mode=single)],
        out_specs=pl.BlockSpec((tm, d), lambda t: (t, 0)), out_shape=jax.ShapeDtypeStruct((n, d), F32),
        compiler_params=_cparams(1), name="mlp",
    )(x, lw["n2"], lw["w_up"], lw["w_down"])


def _inproj_sample_kernel(x_ref, n1_ref, w_ref, bf_ref, qg_ref, kg_ref, gsum_ref, poolw_ref, pscale_ref,
                          dww_ref, dwb_ref, lng_ref, lnb_ref, sp_ref, sc_ref,
                          q_ref, k_ref, v_ref, logf_ref, u_ref, glu_ref, pm_ref, ca_ref, *, start_pos):
    x = x_ref[...]
    n = x.shape[0]
    h = _rms(x, n1_ref[...]).astype(BF16)
    z = jnp.dot(h, w_ref[...], preferred_element_type=F32)
    gsum = gsum_ref[...]
    q_ref[...] = (_head_norm(z[:, C_Q:C_Q + D_ATT], gsum, qg_ref[...]) * (HEAD_DIM ** -0.5)).astype(BF16)
    k_ref[...] = _head_norm(z[:, C_K:C_K + D_ATT], gsum, kg_ref[...])
    v_ref[...] = z[:, C_V:C_V + D_ATT]
    logf_ref[...] = _log_sigmoid(z[:, C_F:C_F + LANES] + bf_ref[...])

    u = z[:, C_POOL:C_POOL + 256]
    u_ref[...] = u
    mixed = []
    for cb in range(2):
        wl = _pool_window_lanes((n, LANES), cb)
        acc = u[:, cb * LANES:(cb + 1) * LANES]
        for j in range(1, POOL_WINDOWS[2 * cb + 1]):
            acc = acc + jnp.where(wl > j, sp_ref[POOL_BUF - j, :, cb * LANES:(cb + 1) * LANES], 0.0)
        cnt = jnp.minimum(start_pos + 1, wl).astype(F32)
        mixed.append(acc / cnt - u[:, cb * LANES:(cb + 1) * LANES])
    mixed = jnp.concatenate(mixed, axis=1).astype(BF16)
    pm_ref[...] = (jnp.dot(mixed, poolw_ref[...], preferred_element_type=F32) * pscale_ref[...]).astype(BF16)

    glu = z[:, C_CONV:C_CONV + 256] * jax.nn.sigmoid(z[:, C_CONV + 256:C_CONV + 512])
    glu_ref[...] = glu
    y = dwb_ref[...] + dww_ref[CONV_BUF:CONV_BUF + 1, :] * glu
    for j in range(CONV_BUF):
        y = y + dww_ref[j:j + 1, :] * sc_ref[j]
    ca_ref[...] = _layer_norm_silu(y, lng_ref[...], lnb_ref[...]).astype(BF16)


def _inproj_sample(x, lw, sp_t, sc_t, start_pos):
    n, d = x.shape
    sds = jax.ShapeDtypeStruct
    out_shape = (sds((n, D_ATT), BF16), sds((n, D_ATT), F32), sds((n, D_ATT), F32), sds((n, LANES), F32),
                 sds((n, 256), F32), sds((n, 256), F32), sds((n, 256), BF16), sds((n, 256), BF16))
    return pl.pallas_call(
        functools.partial(_inproj_sample_kernel, start_pos=start_pos),
        out_shape=out_shape,
        compiler_params=pltpu.CompilerParams(vmem_limit_bytes=VMEM_LIMIT), name="inproj_sample",
    )(x, lw["n1"], lw["w_main"], lw["bf"], lw["qg"], lw["kg"], lw["gsum"], lw["poolw"], lw["pscale"],
      lw["dww"], lw["dwb"], lw["lng"], lw["lnb"], sp_t, sc_t)


def _page_cumsum_kernel(f_ref, mc_ref, o_ref):
    hi, mid, lo = _split3(f_ref[...])
    mc = mc_ref[...]
    o_ref[...] = (jnp.dot(hi.astype(BF16), mc, preferred_element_type=F32)
                  + jnp.dot(mid.astype(BF16), mc, preferred_element_type=F32)
                  + jnp.dot(lo.astype(BF16), mc, preferred_element_type=F32))


def _page_cumsum(logf_pages, layer, mc, tp):
    _, npg, w = logf_pages.shape
    return pl.pallas_call(
        _page_cumsum_kernel, grid=(npg // tp,),
        in_specs=[pl.BlockSpec((None, tp, w), lambda i: (layer, i, 0)), pl.BlockSpec((w, w), lambda i: (0, 0))],
        out_specs=pl.BlockSpec((tp, w), lambda i: (i, 0)), out_shape=jax.ShapeDtypeStruct((npg, w), F32),
        compiler_params=_cparams(1), name="page_cumsum",
    )(logf_pages, mc)


def _attn_sample_kernel(pt_ref, q_ref, kn_ref, vn_ref, fn_ref, *refs, pp, page):
    k_refs, v_refs, c_refs = refs[:pp], refs[pp:2 * pp], refs[2 * pp:3 * pp]
    o_ref = refs[3 * pp]
    m_ref, l_ref, acc_ref, carry_ref = refs[3 * pp + 1:]
    j = pl.program_id(1)
    nj = pl.num_programs(1)

    @pl.when(j == 0)
    def _():
        m_ref[...] = jnp.full((N_HEADS, 1), -jnp.inf, F32)
        l_ref[...] = jnp.zeros((N_HEADS, 1), F32)
        acc_ref[...] = jnp.zeros((N_HEADS, D_ATT), F32)
        carry_ref[...] = jnp.zeros((N_HEADS, 1), F32)

    head_of_lane = lax.broadcasted_iota(jnp.int32, (N_HEADS, D_ATT), 1) // HEAD_DIM
    own = head_of_lane == lax.broadcasted_iota(jnp.int32, (N_HEADS, D_ATT), 0)
    qbd32 = jnp.where(own, jnp.broadcast_to(q_ref[0].astype(F32), (N_HEADS, D_ATT)), 0.0)
    qbd = qbd32.astype(BF16)

    s_list = []
    carry = carry_ref[...]
    for i in range(pp):
        kt = k_refs[i][...].astype(BF16)
        s = jnp.dot(qbd, kt, preferred_element_type=F32)
        cs = c_refs[i][...]
        s_list.append(s - (cs + carry))
        carry = carry + cs[:, page - 1:page]
    carry_ref[...] = carry

    m_old = m_ref[...]
    m_blk = s_list[0]
    for s in s_list[1:]:
        m_blk = jnp.maximum(m_blk, s)
    m_new = jnp.maximum(m_old, jnp.max(m_blk, axis=1, keepdims=True))
    alpha = jnp.exp(m_old - m_new)
    p_sum = jnp.zeros((N_HEADS, page), F32)
    acc = alpha * acc_ref[...]
    for i in range(pp):
        p = jnp.exp(s_list[i] - m_new)
        p_sum = p_sum + p
        acc = acc + lax.dot_general(p.astype(BF16), v_refs[i][...].astype(BF16), (((1,), (1,)), ((), ())),
                                    preferred_element_type=F32)
    l_new = alpha * l_ref[...] + jnp.sum(p_sum, axis=1, keepdims=True)
    m_ref[...] = m_new
    l_ref[...] = l_new
    acc_ref[...] = acc

    @pl.when(j == nj - 1)
    def _():
        kn = jnp.broadcast_to(kn_ref[0], (N_HEADS, D_ATT)).astype(BF16).astype(F32)
        vn = jnp.broadcast_to(vn_ref[0], (N_HEADS, D_ATT)).astype(BF16).astype(F32)
        fn = fn_ref[0]
        s_new = jnp.sum(qbd32 * kn, axis=1, keepdims=True) - (carry + fn)
        m_fin = jnp.maximum(m_new, s_new)
        a_fin = jnp.exp(m_new - m_fin)
        p_new = jnp.exp(s_new - m_fin)
        l_fin = a_fin * l_new + p_new
        acc_fin = a_fin * acc + p_new.astype(BF16).astype(F32) * vn
        o = jnp.where(own, acc_fin / l_fin, 0.0)
        o_ref[0] = jnp.sum(o, axis=0, keepdims=True).astype(BF16)


def _attn_sample(page_table, q, k_new, v_new, logf_new, cache_k, cache_v, layer, cs_l, pp):
    bd, n_pages = page_table.shape
    page = cache_k.shape[3]
    nj = n_pages // pp
    tok = lambda w: pl.BlockSpec((1, 1, w), lambda b, j, pt: (b, 0, 0))
    kv_spec = lambda i: pl.BlockSpec((None, None, D_ATT, page),
                                     lambda b, j, pt, i=i: (layer, pt[b, j * pp + i], 0, 0))
    c_spec = lambda i: pl.BlockSpec((None, N_HEADS, page), lambda b, j, pt, i=i: (pt[b, j * pp + i], 0, 0))
    in_specs = ([tok(D_ATT), tok(D_ATT), tok(D_ATT), pl.BlockSpec((1, N_HEADS, 1), lambda b, j, pt: (b, 0, 0))]
                + [kv_spec(i) for i in range(pp)] + [kv_spec(i) for i in range(pp)] + [c_spec(i) for i in range(pp)])
    grid_spec = pltpu.PrefetchScalarGridSpec(
        num_scalar_prefetch=1, grid=(bd, nj), in_specs=in_specs,
        out_specs=pl.BlockSpec((1, 1, D_ATT), lambda b, j, pt: (b, 0, 0)),
        scratch_shapes=[pltpu.VMEM((N_HEADS, 1), F32), pltpu.VMEM((N_HEADS, 1), F32),
                        pltpu.VMEM((N_HEADS, D_ATT), F32), pltpu.VMEM((N_HEADS, 1), F32)])
    out = pl.pallas_call(
        functools.partial(_attn_sample_kernel, pp=pp, page=page),
        grid_spec=grid_spec, out_shape=jax.ShapeDtypeStruct((bd, 1, D_ATT), BF16),
        compiler_params=_cparams(2), name="attn_sample",
    )(page_table, q[:, None, :], k_new[:, None, :], v_new[:, None, :], logf_new[:, :N_HEADS, None],
      *([cache_k] * pp), *([cache_v] * pp), *([cs_l] * pp))
    return out[:, 0, :]


def _layer_weights(l, tm, norm1_g, w_in, b_f, b_gate, q_gain, k_gain, pool_w, pool_scale, w_pool_up, w_att_up,
                   dw_w, dw_b, conv_ln_g, conv_ln_b, w_conv_out, w_out, norm2_g, w_mlp_up, w_mlp_down):
    d = w_in.shape[1]
    w = w_in[l]
    o_f = 3 * D_ATT
    o_pool = o_f + N_HEADS
    o_conv = o_pool + 256
    o_gate = o_conv + 512
    w_main = jnp.concatenate([w[:, :o_f], w[:, o_pool:o_gate], w[:, o_f:o_pool],
                              jnp.zeros((d, LANES - N_HEADS), F32)], axis=1).astype(BF16)
    grp = jnp.arange(256) // HEAD_DIM
    poolw = jnp.zeros((256, 256), F32)
    for g in range(4):
        poolw = poolw.at[g * 64:(g + 1) * 64, g * 64:(g + 1) * 64].set(pool_w[l, g])
    return dict(
        n1=norm1_g[l][None, :], w_main=w_main, w_gate=w[:, o_gate:].astype(BF16),
        bf=jnp.pad(b_f[l], (0, LANES - N_HEADS))[None, :], bg=b_gate[l],
        qg=jnp.tile(q_gain[l], N_HEADS)[None, :], kg=jnp.tile(k_gain[l], N_HEADS)[None, :],
        gsum=(grp[:, None] == grp[None, :]).astype(BF16),
        ltri=(jnp.arange(tm)[:, None] >= jnp.arange(tm)[None, :]).astype(BF16),
        poolw=poolw.astype(BF16), pscale=pool_scale[l][None, :],
        dww=jnp.pad(dw_w[l], ((0, CONV_HIST - CONV_WIDTH), (0, 0))), dwb=dw_b[l][None, :],
        lng=conv_ln_g[l][None, :], lnb=conv_ln_b[l][None, :],
        w_att_up=w_att_up[l].astype(BF16), w_pool_up=w_pool_up[l].astype(BF16),
        w_conv_out=w_conv_out[l].astype(BF16), w_out=w_out[l].astype(BF16),
        n2=norm2_g[l][None, :], w_up=w_mlp_up[l].astype(BF16), w_down=w_mlp_down[l].astype(BF16))


def kernel(x_prompt, x_sample, cache_k, cache_v, cache_logf, state_pool, state_conv, page_table,
           norm1_g, w_in, b_f, b_gate, q_gain, k_gain, pool_w, pool_scale, w_pool_up, w_att_up,
           dw_w, dw_b, conv_ln_g, conv_ln_b, w_conv_out, w_out, norm2_g, w_mlp_up, w_mlp_down):
    B, T, D = x_prompt.shape
    bd = x_sample.shape[0]
    depth = w_in.shape[0]
    n_pool_pages, page = cache_k.shape[1], cache_k.shape[2]
    n_pages = page_table.shape[1]
    past_len = n_pages * page
    tm = min(512, T)
    pp = min(16, n_pages)

    mc = (jnp.arange(page)[:, None] <= jnp.arange(page)[None, :]).astype(BF16)
    n_rows = n_pool_pages * N_HEADS
    tp = 4096 if n_rows % 4096 == 0 else n_rows
    ck_flat = jnp.transpose(cache_k, (0, 1, 3, 4, 2)).reshape(depth, n_pool_pages, D_ATT, page)
    cv_flat = jnp.transpose(cache_v, (0, 1, 3, 4, 2)).reshape(depth, n_pool_pages, D_ATT, page)
    logf_flat = jnp.transpose(cache_logf, (0, 1, 3, 2)).reshape(depth, n_rows, page)

    xp = x_prompt
    xs = x_sample.reshape(bd, D)
    outs = [[] for _ in range(7)]
    stacks = ()
    for l in range(depth):
        lw = _layer_weights(l, tm, norm1_g, w_in, b_f, b_gate, q_gain, k_gain, pool_w, pool_scale, w_pool_up,
                            w_att_up, dw_w, dw_b, conv_ln_g, conv_ln_b, w_conv_out, w_out, norm2_g,
                            w_mlp_up, w_mlp_down)
        qt, ka, vtb, kt_all, vt_all, ft_all, pm, ca, ptail, ctail = _inproj_prompt(xp, lw, tm, l, depth, stacks)
        stacks = (kt_all, vt_all, ft_all)
        att_t = _attn_prompt(qt, ka, vtb, tm)
        x1 = _merge(xp, att_t, pm, ca, lw, tm)
        xp = _mlp(x1.reshape(B * T, D), lw, tm).reshape(B, T, D)
        sp_t = jnp.transpose(state_pool[l], (1, 0, 2))
        sc_t = jnp.transpose(state_conv[l], (1, 0, 2))
        qs, ks, vs, fs, us, gs, pms, cas = _inproj_sample(xs, lw, sp_t, sc_t, past_len)
        cs_l = _page_cumsum(logf_flat, l, mc, tp)
        att_s = _attn_sample(page_table, qs, ks, vs, fs, ck_flat, cv_flat, l,
                             cs_l.reshape(n_pool_pages, N_HEADS, page), pp)
        xs1 = _merge(xs[None], att_s.T[None], pms[None], cas[None], lw, bd)[0]
        xs = _mlp(xs1, lw, bd)

        outs[0].append(ptail[:, POOL_HIST - POOL_BUF:])
        outs[1].append(ctail[:, CONV_HIST - CONV_BUF:])
        outs[2].append(ks.reshape(bd, 1, N_HEADS, HEAD_DIM))
        outs[3].append(vs.reshape(bd, 1, N_HEADS, HEAD_DIM))
        outs[4].append(fs[:, None, :N_HEADS])
        outs[5].append(jnp.concatenate([state_pool[l][:, 1:], us[:, None, :]], axis=1))
        outs[6].append(jnp.concatenate([state_conv[l][:, 1:], gs[:, None, :]], axis=1))
    kt_all, vt_all, ft_all = stacks
    k_prompt = jnp.transpose(kt_all.reshape(depth, B, N_HEADS, HEAD_DIM, T), (0, 1, 4, 2, 3))
    v_prompt = jnp.transpose(vt_all.reshape(depth, B, N_HEADS, HEAD_DIM, T), (0, 1, 4, 2, 3))
    f_prompt = jnp.transpose(ft_all, (0, 1, 3, 2))
    st = [jnp.stack(o) for o in outs]
    return (xp, xs.reshape(bd, 1, D), k_prompt, v_prompt, f_prompt) + tuple(st)
```

```python
import functools

import jax
import jax.numpy as jnp
from jax import lax
from jax.experimental import pallas as pl
from jax.experimental.pallas import tpu as pltpu

F32 = jnp.float32
BF16 = jnp.bfloat16

N_HEADS = 8
HEAD_DIM = 64
D_ATT = N_HEADS * HEAD_DIM
N_PAIRS = N_HEADS // 2
POOL_WINDOWS = (2, 4, 8, 16)
POOL_BUF = max(POOL_WINDOWS) - 1
POOL_HIST = 32
CONV_WIDTH = 31
CONV_BUF = CONV_WIDTH - 1
CONV_HIST = 32
NORM_EPS = 1e-6
LN_EPS = 1e-5
LOG2E = 1.4426950408889634
S_CHUNK = 512
LANES = 128
VMEM_LIMIT = 56 * 1024 * 1024

C_Q, C_K, C_V, C_POOL, C_CONV, C_F = 0, 512, 1024, 1536, 1792, 2304
D_MAIN = C_F + LANES


def _cparams(n_axes):
    return pltpu.CompilerParams(dimension_semantics=("arbitrary",) * n_axes, vmem_limit_bytes=VMEM_LIMIT)


def _split3(x):
    hi = x.astype(BF16).astype(F32)
    r = x - hi
    mid = r.astype(BF16).astype(F32)
    lo = (r - mid).astype(BF16).astype(F32)
    return hi, mid, lo


def _rms(x, g):
    return x * lax.rsqrt(jnp.mean(x * x, axis=-1, keepdims=True) + NORM_EPS) * g


def _log_sigmoid(x):
    return jnp.minimum(x, 0.0) - jnp.log1p(jnp.exp(-jnp.abs(x)))


def _head_norm(a, gsum, gain):
    outs = []
    for c in range(2):
        blk = a[:, c * 256:(c + 1) * 256]
        sq = blk * blk
        hi = sq.astype(BF16)
        lo = (sq - hi.astype(F32)).astype(BF16)
        ss = jnp.dot(hi, gsum, preferred_element_type=F32) + jnp.dot(lo, gsum, preferred_element_type=F32)
        outs.append(blk * lax.rsqrt(ss * (1.0 / HEAD_DIM) + NORM_EPS))
    return jnp.concatenate(outs, axis=1) * gain


def _pool_window_lanes(shape, col_block):
    lane = lax.broadcasted_iota(jnp.int32, shape, len(shape) - 1)
    lo_w, hi_w = POOL_WINDOWS[2 * col_block], POOL_WINDOWS[2 * col_block + 1]
    return jnp.where(lane < 64, lo_w, hi_w)


def _layer_norm_silu(y, g, b):
    mu = jnp.mean(y, axis=-1, keepdims=True)
    d = y - mu
    var = jnp.mean(d * d, axis=-1, keepdims=True)
    yn = d * lax.rsqrt(var + LN_EPS) * g + b
    return yn * jax.nn.sigmoid(yn)


def _inproj_prompt_kernel(x_ref, n1_ref, w_ref, bf_ref, qg_ref, kg_ref, gsum_ref, ltri_ref, poolw_ref,
                          pscale_ref, dww_ref, dwb_ref, lng_ref, lnb_ref,
                          *rest, tm, n_prev):
    (qt_ref, ka_ref, vtb_ref, kt_ref, vt_ref, logft_ref, pm_ref, ca_ref, ptail_ref, ctail_ref,
     uh_ref, gh_ref, sh_ref, carry_ref) = rest[n_prev:]
    t = pl.program_id(1)

    @pl.when(t == 0)
    def _():
        uh_ref[0:POOL_HIST, :] = jnp.zeros((POOL_HIST, 256), F32)
        gh_ref[0:CONV_HIST, :] = jnp.zeros((CONV_HIST, 256), F32)
        carry_ref[...] = jnp.zeros((1, LANES), F32)

    x = x_ref[0]
    h = _rms(x, n1_ref[...]).astype(BF16)
    z = jnp.dot(h, w_ref[...], preferred_element_type=F32)

    gsum = gsum_ref[...]
    qn = _head_norm(z[:, C_Q:C_Q + D_ATT], gsum, qg_ref[...]) * (HEAD_DIM ** -0.5 * LOG2E)
    kn = _head_norm(z[:, C_K:C_K + D_ATT], gsum, kg_ref[...])
    qt = qn.T
    vt = z[:, C_V:C_V + D_ATT].T
    kt_ref[0] = kn.T
    vt_ref[0] = vt

    lane = lax.broadcasted_iota(jnp.int32, (tm, LANES), 1)
    logf = jnp.where(lane < N_HEADS, _log_sigmoid(z[:, C_F:C_F + LANES] + bf_ref[...]), 0.0)
    logft_ref[0] = logf.T[0:N_HEADS, :]
    f_hi, f_mid, f_lo = _split3(logf)
    f3 = (f_hi + pltpu.roll(f_mid, 8, 1) + pltpu.roll(f_lo, 16, 1)).astype(BF16)
    cs3 = jnp.dot(ltri_ref[...], f3, preferred_element_type=F32)
    c = cs3 + pltpu.roll(cs3, LANES - 8, 1) + pltpu.roll(cs3, LANES - 16, 1) + carry_ref[...]
    c = jnp.where(lane < N_HEADS, c, 0.0)
    carry_ref[...] = c[tm - 1:tm, :]
    c2 = c * LOG2E
    c_hi, c_mid, c_lo = _split3(c2)
    ct_hi, ct_mid, ct_lo = _split3(c2.T[0:N_HEADS, :])

    row8 = lax.broadcasted_iota(jnp.int32, (8, tm), 0)
    pad_rows = jnp.zeros((HEAD_DIM - 8, tm), F32)
    for hd in range(N_HEADS):
        p, odd = hd // 2, hd % 2
        base = 0 if odd else 64
        ch, cm, cl = c_hi[:, hd:hd + 1], c_mid[:, hd:hd + 1], c_lo[:, hd:hd + 1]
        ones_k = (lane >= base) & (lane < base + 3)
        ek = jnp.where(lane == base + 3, -ch, jnp.where(lane == base + 4, -cm, jnp.where(lane == base + 5, -cl,
                       jnp.where(ones_k, 1.0, 0.0))))
        val = (lane >= 64) if odd else (lane < 64)
        ka_ref[0, hd] = jnp.where(val, kn[:, p * LANES:(p + 1) * LANES], ek).astype(BF16)
        eq = jnp.where(row8 == 0, ct_hi[hd:hd + 1, :], jnp.where(row8 == 1, ct_mid[hd:hd + 1, :],
                       jnp.where(row8 == 2, ct_lo[hd:hd + 1, :], jnp.where(row8 < 6, 1.0, 0.0))))
        qh = qt[hd * HEAD_DIM:(hd + 1) * HEAD_DIM, :]
        parts = [eq, pad_rows, qh] if odd else [qh, eq, pad_rows]
        qt_ref[0, hd] = jnp.concatenate(parts, axis=0).astype(BF16)
        vtb_ref[0, hd, 0] = vt[hd * HEAD_DIM:(hd + 1) * HEAD_DIM, :].astype(BF16)

    u = z[:, C_POOL:C_POOL + 256]
    uh_ref[POOL_HIST:POOL_HIST + tm, :] = u
    pos = t * tm + lax.broadcasted_iota(jnp.int32, (tm, LANES), 0)
    n_ext = POOL_HIST + tm
    mixed = []
    for cb in range(2):
        wl = _pool_window_lanes((tm, LANES), cb)
        e = uh_ref[:, cb * LANES:(cb + 1) * LANES]
        s2 = e[8:n_ext] + e[7:n_ext - 1]
        s4 = s2[8:] + s2[6:n_ext - 10]
        if cb == 0:
            lo, hi = s2[24:24 + tm], s4[16:16 + tm]
        else:
            s8 = s4[8:] + s4[4:n_ext - 20]
            s16 = s8[8:] + s8[0:n_ext - 32]
            lo, hi = s8[8:8 + tm], s16
        lane_b = lax.broadcasted_iota(jnp.int32, (tm, LANES), 1)
        acc = jnp.where(lane_b < 64, lo, hi)
        cnt = jnp.minimum(pos + 1, wl).astype(F32)
        mixed.append(acc / cnt - u[:, cb * LANES:(cb + 1) * LANES])
    mixed = jnp.concatenate(mixed, axis=1).astype(BF16)
    pm = jnp.dot(mixed, poolw_ref[...], preferred_element_type=F32) * pscale_ref[...]
    pm_ref[0] = pm.astype(BF16)
    ptail_ref[0] = uh_ref[tm:tm + POOL_HIST, :]
    uh_ref[0:POOL_HIST, :] = uh_ref[tm:tm + POOL_HIST, :]

    glu = z[:, C_CONV:C_CONV + 256] * jax.nn.sigmoid(z[:, C_CONV + 256:C_CONV + 512])
    gh_ref[CONV_HIST:CONV_HIST + tm, :] = glu
    y = jnp.zeros((tm, 256), F32) + dwb_ref[...]
    first = CONV_HIST - CONV_BUF
    for r in range(8):
        offs = [o for o in range(first, first + CONV_WIDTH) if o % 8 == r]
        a_max = max(offs) // 8
        if r:
            sh_ref[0:tm + 8 * a_max, :] = gh_ref[r:r + tm + 8 * a_max, :]
        src = sh_ref if r else gh_ref
        for o in offs:
            a = o // 8
            y = y + dww_ref[o - first:o - first + 1, :] * src[8 * a:8 * a + tm, :]
    ca_ref[0] = _layer_norm_silu(y, lng_ref[...], lnb_ref[...]).astype(BF16)
    ctail_ref[0] = gh_ref[tm:tm + CONV_HIST, :]
    gh_ref[0:CONV_HIST, :] = gh_ref[tm:tm + CONV_HIST, :]


def _inproj_prompt(x, lw, tm, layer, depth, prev):
    B, T, D = x.shape
    nt = T // tm
    const = lambda shape: pl.BlockSpec(shape, lambda b, t: (0,) * len(shape))
    row = lambda w: pl.BlockSpec((1, tm, w), lambda b, t: (b, t, 0))
    tail = lambda r: pl.BlockSpec((1, r, 256), lambda b, t: (b, 0, 0))
    stack = lambda r: pl.BlockSpec((None, 1, r, tm), lambda b, t: (layer, b, 0, t))
    out_shape = (
        jax.ShapeDtypeStruct((B, N_HEADS, LANES, T), BF16),
        jax.ShapeDtypeStruct((B, N_HEADS, T, LANES), BF16),
        jax.ShapeDtypeStruct((B, N_HEADS, nt, HEAD_DIM, tm), BF16),
        jax.ShapeDtypeStruct((depth, B, D_ATT, T), F32),
        jax.ShapeDtypeStruct((depth, B, D_ATT, T), F32),
        jax.ShapeDtypeStruct((depth, B, N_HEADS, T), F32),
        jax.ShapeDtypeStruct((B, T, 256), BF16),
        jax.ShapeDtypeStruct((B, T, 256), BF16),
        jax.ShapeDtypeStruct((B, POOL_HIST, 256), F32),
        jax.ShapeDtypeStruct((B, CONV_HIST, 256), F32),
    )
    out_specs = (pl.BlockSpec((1, N_HEADS, LANES, tm), lambda b, t: (b, 0, 0, t)),
                 pl.BlockSpec((1, N_HEADS, tm, LANES), lambda b, t: (b, 0, t, 0)),
                 pl.BlockSpec((1, N_HEADS, 1, HEAD_DIM, tm), lambda b, t: (b, 0, t, 0, 0)),
                 stack(D_ATT), stack(D_ATT), stack(N_HEADS), row(256), row(256), tail(POOL_HIST), tail(CONV_HIST))
    in_specs = [row(D), const((1, D)), const((D, D_MAIN)), const((1, LANES)), const((1, D_ATT)), const((1, D_ATT)),
                const((256, 256)), const((tm, tm)), const((256, 256)), const((1, 256)), const((CONV_HIST, 256)),
                const((1, 256)), const((1, 256)), const((1, 256))]
    n_in = len(in_specs)
    in_specs += [pl.BlockSpec(memory_space=pl.ANY)] * len(prev)
    aliases = {n_in + i: 3 + i for i in range(len(prev))}
    return pl.pallas_call(
        functools.partial(_inproj_prompt_kernel, tm=tm, n_prev=len(prev)),
        grid=(B, nt), in_specs=in_specs, out_specs=out_specs, out_shape=out_shape,
        scratch_shapes=[pltpu.VMEM((POOL_HIST + tm, 256), F32), pltpu.VMEM((CONV_HIST + tm, 256), F32),
                        pltpu.VMEM((CONV_HIST + tm, 256), F32), pltpu.VMEM((1, LANES), F32)],
        input_output_aliases=aliases,
        compiler_params=_cparams(2), name="inproj_prompt",
    )(x, lw["n1"], lw["w_main"], lw["bf"], lw["qg"], lw["kg"], lw["gsum"], lw["ltri"], lw["poolw"],
      lw["pscale"], lw["dww"], lw["dwb"], lw["lng"], lw["lnb"], *prev)


def _attn_prompt_kernel(qt_ref, ka_ref, vt_ref, o_ref, s_ref, p_ref, m_ref, l_ref, acc_ref, *, tq):
    i = pl.program_id(2)
    n_chunk = tq // S_CHUNK
    for hh in range(2):
        m_ref[hh] = jnp.full((1, tq), -jnp.inf, F32)
        l_ref[hh] = jnp.zeros((1, tq), F32)
        acc_ref[hh] = jnp.zeros((HEAD_DIM, tq), F32)

    def score(hh, j, masked):
        start = pl.multiple_of(j * tq, tq)
        for r in range(n_chunk):
            k = ka_ref[0, hh, pl.ds(start + r * S_CHUNK, S_CHUNK), :]
            st = jnp.dot(k, qt_ref[0, hh], preferred_element_type=F32)
            if masked:
                key = lax.broadcasted_iota(jnp.int32, (S_CHUNK, tq), 0) + r * S_CHUNK
                qry = lax.broadcasted_iota(jnp.int32, (S_CHUNK, tq), 1)
                st = jnp.where(key <= qry, st, -jnp.inf)
            s_ref[hh, r * S_CHUNK:(r + 1) * S_CHUNK, :] = st

    def softmax(hh):
        m_old = m_ref[hh]
        m_new = jnp.maximum(m_old, jnp.max(s_ref[hh], axis=0, keepdims=True))
        p = jnp.exp2(s_ref[hh] - m_new)
        alpha = jnp.exp2(m_old - m_new)
        l_ref[hh] = alpha * l_ref[hh] + jnp.sum(p, axis=0, keepdims=True)
        m_ref[hh] = m_new
        p_ref[hh] = p.astype(BF16)
        return alpha

    def pv(hh, jv, alpha):
        acc_ref[hh] = alpha * acc_ref[hh] + jnp.dot(vt_ref[0, hh, jv], p_ref[hh],
                                                   preferred_element_type=F32)

    for hh in range(2):
        score(hh, i, True)

    def body(t, carry):
        prev = jnp.where(t == 0, i, t - 1)
        for hh in range(2):
            alpha = softmax(hh)
            score(hh, t, False)
            pv(hh, prev, alpha)
        return carry

    lax.fori_loop(0, i, body, 0)
    last = jnp.maximum(i - 1, 0)
    for hh in range(2):
        pv(hh, last, softmax(hh))
    o = jnp.concatenate([acc_ref[0] / l_ref[0], acc_ref[1] / l_ref[1]], axis=0)
    o_ref[0] = o.astype(BF16)


def _attn_prompt(qt, ka, vtb, tq):
    B, _, _, T = qt.shape
    nq = T // tq
    return pl.pallas_call(
        functools.partial(_attn_prompt_kernel, tq=tq),
        grid=(B, N_PAIRS, nq),
        in_specs=[pl.BlockSpec((1, 2, LANES, tq), lambda b, p, i: (b, p, 0, i)),
                  pl.BlockSpec((1, 2, T, LANES), lambda b, p, i: (b, p, 0, 0)),
                  pl.BlockSpec((1, 2, nq, HEAD_DIM, tq), lambda b, p, i: (b, p, 0, 0, 0))],
        out_specs=pl.BlockSpec((1, 2 * HEAD_DIM, tq), lambda b, p, i: (b, p, i)),
        out_shape=jax.ShapeDtypeStruct((B, D_ATT, T), BF16),
        scratch_shapes=[pltpu.VMEM((2, tq, tq), F32), pltpu.VMEM((2, tq, tq), BF16),
                        pltpu.VMEM((2, 1, tq), F32), pltpu.VMEM((2, 1, tq), F32),
                        pltpu.VMEM((2, HEAD_DIM, tq), F32)],
        compiler_params=_cparams(3), name="attn_prompt",
    )(qt, ka, vtb)


def _merge_kernel(x_ref, att_ref, pm_ref, ca_ref, n1_ref, wg_ref, bg_ref, wau_ref, wpu_ref, wco_ref, wo_ref, o_ref):
    x = x_ref[0]
    d = x.shape[1]
    h = _rms(x, n1_ref[...]).astype(BF16)
    branches = (lax.dot_general(att_ref[0], wau_ref[...], (((0,), (0,)), ((), ())), preferred_element_type=F32),
                jnp.dot(pm_ref[0], wpu_ref[...], preferred_element_type=F32),
                jnp.dot(ca_ref[0], wco_ref[...], preferred_element_type=F32))
    merged = jnp.zeros_like(x)
    for br in range(3):
        gl = jnp.dot(h, wg_ref[:, br * d:(br + 1) * d], preferred_element_type=F32) + bg_ref[br:br + 1, :]
        merged = merged + jax.nn.sigmoid(gl) * branches[br]
    o_ref[0] = x + jnp.dot(merged.astype(BF16), wo_ref[...], preferred_element_type=F32)


def _merge(x, att_t, pm, ca, lw, tm):
    B, T, d = x.shape
    const = lambda shape: pl.BlockSpec(shape, lambda b, t: (0, 0))
    row = lambda w: pl.BlockSpec((1, tm, w), lambda b, t: (b, t, 0))
    return pl.pallas_call(
        _merge_kernel, grid=(B, T // tm),
        in_specs=[row(d), pl.BlockSpec((1, D_ATT, tm), lambda b, t: (b, 0, t)), row(256), row(256),
                  const((1, d)), const((d, 3 * d)), const((3, d)),
                  const((D_ATT, d)), const((256, d)), const((256, d)), const((d, d))],
        out_specs=row(d), out_shape=jax.ShapeDtypeStruct((B, T, d), F32),
        compiler_params=_cparams(2), name="merge",
    )(x, att_t, pm, ca, lw["n1"], lw["w_gate"], lw["bg"], lw["w_att_up"], lw["w_pool_up"], lw["w_conv_out"], lw["w_out"])


def _mlp_kernel(x_ref, n2_ref, wu_ref, wd_ref, o_ref):
    x = x_ref[...]
    h = _rms(x, n2_ref[...]).astype(BF16)
    a = jnp.maximum(jnp.dot(h, wu_ref[...], preferred_element_type=F32), 0.0)
    hid = (a * a).astype(BF16)
    o_ref[...] = x + jnp.dot(hid, wd_ref[...], preferred_element_type=F32)


def _mlp(x, lw, tm):
    n, d = x.shape
    dff = lw["w_up"].shape[1]
    single = pl.Buffered(1)
    return pl.pallas_call(
        _mlp_kernel, grid=(n // tm,),
        in_specs=[pl.BlockSpec((tm, d), lambda t: (t, 0)), pl.BlockSpec((1, d), lambda t: (0, 0)),
                  pl.BlockSpec((d, dff), lambda t: (0, 0), pipeline_mode=single),
                  pl.BlockSpec((dff, d), lambda t: (0, 0), pipeline_mode=single)],
        out_specs=pl.BlockSpec((tm, d), lambda t: (t, 0)), out_shape=jax.ShapeDtypeStruct((n, d), F32),
        compiler_params=_cparams(1), name="mlp",
    )(x, lw["n2"], lw["w_up"], lw["w_down"])


def _inproj_sample_kernel(x_ref, n1_ref, w_ref, bf_ref, qg_ref, kg_ref, gsum_ref, poolw_ref, pscale_ref,
                          dww_ref, dwb_ref, lng_ref, lnb_ref, sp_ref, sc_ref,
                          q_ref, k_ref, v_ref, logf_ref, u_ref, glu_ref, pm_ref, ca_ref, *, start_pos):
    x = x_ref[...]
    n = x.shape[0]
    h = _rms(x, n1_ref[...]).astype(BF16)
    z = jnp.dot(h, w_ref[...], preferred_element_type=F32)
    gsum = gsum_ref[...]
    q_ref[...] = (_head_norm(z[:, C_Q:C_Q + D_ATT], gsum, qg_ref[...]) * (HEAD_DIM ** -0.5)).astype(BF16)
    k_ref[...] = _head_norm(z[:, C_K:C_K + D_ATT], gsum, kg_ref[...])
    v_ref[...] = z[:, C_V:C_V + D_ATT]
    logf_ref[...] = _log_sigmoid(z[:, C_F:C_F + LANES] + bf_ref[...])

    u = z[:, C_POOL:C_POOL + 256]
    u_ref[...] = u
    mixed = []
    for cb in range(2):
        wl = _pool_window_lanes((n, LANES), cb)
        acc = u[:, cb * LANES:(cb + 1) * LANES]
        for j in range(1, POOL_WINDOWS[2 * cb + 1]):
            acc = acc + jnp.where(wl > j, sp_ref[POOL_BUF - j, :, cb * LANES:(cb + 1) * LANES], 0.0)
        cnt = jnp.minimum(start_pos + 1, wl).astype(F32)
        mixed.append(acc / cnt - u[:, cb * LANES:(cb + 1) * LANES])
    mixed = jnp.concatenate(mixed, axis=1).astype(BF16)
    pm_ref[...] = (jnp.dot(mixed, poolw_ref[...], preferred_element_type=F32) * pscale_ref[...]).astype(BF16)

    glu = z[:, C_CONV:C_CONV + 256] * jax.nn.sigmoid(z[:, C_CONV + 256:C_CONV + 512])
    glu_ref[...] = glu
    y = dwb_ref[...] + dww_ref[CONV_BUF:CONV_BUF + 1, :] * glu
    for j in range(CONV_BUF):
        y = y + dww_ref[j:j + 1, :] * sc_ref[j]
    ca_ref[...] = _layer_norm_silu(y, lng_ref[...], lnb_ref[...]).astype(BF16)


def _inproj_sample(x, lw, sp_t, sc_t, start_pos):
    n, d = x.shape
    sds = jax.ShapeDtypeStruct
    out_shape = (sds((n, D_ATT), BF16), sds((n, D_ATT), F32), sds((n, D_ATT), F32), sds((n, LANES), F32),
                 sds((n, 256), F32), sds((n, 256), F32), sds((n, 256), BF16), sds((n, 256), BF16))
    return pl.pallas_call(
        functools.partial(_inproj_sample_kernel, start_pos=start_pos),
        out_shape=out_shape,
        compiler_params=pltpu.CompilerParams(vmem_limit_bytes=VMEM_LIMIT), name="inproj_sample",
    )(x, lw["n1"], lw["w_main"], lw["bf"], lw["qg"], lw["kg"], lw["gsum"], lw["poolw"], lw["pscale"],
      lw["dww"], lw["dwb"], lw["lng"], lw["lnb"], sp_t, sc_t)


def _page_cumsum_kernel(f_ref, mc_ref, o_ref):
    hi, mid, lo = _split3(f_ref[...])
    mc = mc_ref[...]
    o_ref[...] = (jnp.dot(hi.astype(BF16), mc, preferred_element_type=F32)
                  + jnp.dot(mid.astype(BF16), mc, preferred_element_type=F32)
                  + jnp.dot(lo.astype(BF16), mc, preferred_element_type=F32))


def _page_cumsum(logf_pages, layer, mc, tp):
    _, npg, w = logf_pages.shape
    return pl.pallas_call(
        _page_cumsum_kernel, grid=(npg // tp,),
        in_specs=[pl.BlockSpec((None, tp, w), lambda i: (layer, i, 0)), pl.BlockSpec((w, w), lambda i: (0, 0))],
        out_specs=pl.BlockSpec((tp, w), lambda i: (i, 0)), out_shape=jax.ShapeDtypeStruct((npg, w), F32),
        compiler_params=_cparams(1), name="page_cumsum",
    )(logf_pages, mc)


def _attn_sample_kernel(pt_ref, q_ref, kn_ref, vn_ref, fn_ref, *refs, pp, page):
    k_refs, v_refs, c_refs = refs[:pp], refs[pp:2 * pp], refs[2 * pp:3 * pp]
    o_ref = refs[3 * pp]
    m_ref, l_ref, acc_ref, carry_ref = refs[3 * pp + 1:]
    j = pl.program_id(1)
    nj = pl.num_programs(1)

    @pl.when(j == 0)
    def _():
        m_ref[...] = jnp.full((N_HEADS, 1), -jnp.inf, F32)
        l_ref[...] = jnp.zeros((N_HEADS, 1), F32)
        acc_ref[...] = jnp.zeros((N_HEADS, D_ATT), F32)
        carry_ref[...] = jnp.zeros((N_HEADS, 1), F32)

    head_of_lane = lax.broadcasted_iota(jnp.int32, (N_HEADS, D_ATT), 1) // HEAD_DIM
    own = head_of_lane == lax.broadcasted_iota(jnp.int32, (N_HEADS, D_ATT), 0)
    qbd32 = jnp.where(own, jnp.broadcast_to(q_ref[0].astype(F32), (N_HEADS, D_ATT)), 0.0)
    qbd = qbd32.astype(BF16)

    s_list = []
    carry = carry_ref[...]
    for i in range(pp):
        kt = k_refs[i][...].astype(BF16)
        s = jnp.dot(qbd, kt, preferred_element_type=F32)
        cs = c_refs[i][...]
        s_list.append(s - (cs + carry))
        carry = carry + cs[:, page - 1:page]
    carry_ref[...] = carry

    m_old = m_ref[...]
    m_blk = s_list[0]
    for s in s_list[1:]:
        m_blk = jnp.maximum(m_blk, s)
    m_new = jnp.maximum(m_old, jnp.max(m_blk, axis=1, keepdims=True))
    alpha = jnp.exp(m_old - m_new)
    p_sum = jnp.zeros((N_HEADS, page), F32)
    acc = alpha * acc_ref[...]
    for i in range(pp):
        p = jnp.exp(s_list[i] - m_new)
        p_sum = p_sum + p
        acc = acc + lax.dot_general(p.astype(BF16), v_refs[i][...].astype(BF16), (((1,), (1,)), ((), ())),
                                    preferred_element_type=F32)
    l_new = alpha * l_ref[...] + jnp.sum(p_sum, axis=1, keepdims=True)
    m_ref[...] = m_new
    l_ref[...] = l_new
    acc_ref[...] = acc

    @pl.when(j == nj - 1)
    def _():
        kn = jnp.broadcast_to(kn_ref[0], (N_HEADS, D_ATT)).astype(BF16).astype(F32)
        vn = jnp.broadcast_to(vn_ref[0], (N_HEADS, D_ATT)).astype(BF16).astype(F32)
        fn = fn_ref[0]
        s_new = jnp.sum(qbd32 * kn, axis=1, keepdims=True) - (carry + fn)
        m_fin = jnp.maximum(m_new, s_new)
        a_fin = jnp.exp(m_new - m_fin)
        p_new = jnp.exp(s_new - m_fin)
        l_fin = a_fin * l_new + p_new
        acc_fin = a_fin * acc + p_new.astype(BF16).astype(F32) * vn
        o = jnp.where(own, acc_fin / l_fin, 0.0)
        o_ref[0] = jnp.sum(o, axis=0, keepdims=True).astype(BF16)


def _attn_sample(page_table, q, k_new, v_new, logf_new, cache_k, cache_v, layer, cs_l, pp):
    bd, n_pages = page_table.shape
    page = cache_k.shape[3]
    nj = n_pages // pp
    tok = lambda w: pl.BlockSpec((1, 1, w), lambda b, j, pt: (b, 0, 0))
    kv_spec = lambda i: pl.BlockSpec((None, None, D_ATT, page),
                                     lambda b, j, pt, i=i: (layer, pt[b, j * pp + i], 0, 0))
    c_spec = lambda i: pl.BlockSpec((None, N_HEADS, page), lambda b, j, pt, i=i: (pt[b, j * pp + i], 0, 0))
    in_specs = ([tok(D_ATT), tok(D_ATT), tok(D_ATT), pl.BlockSpec((1, N_HEADS, 1), lambda b, j, pt: (b, 0, 0))]
                + [kv_spec(i) for i in range(pp)] + [kv_spec(i) for i in range(pp)] + [c_spec(i) for i in range(pp)])
    grid_spec = pltpu.PrefetchScalarGridSpec(
        num_scalar_prefetch=1, grid=(bd, nj), in_specs=in_specs,
        out_specs=pl.BlockSpec((1, 1, D_ATT), lambda b, j, pt: (b, 0, 0)),
        scratch_shapes=[pltpu.VMEM((N_HEADS, 1), F32), pltpu.VMEM((N_HEADS, 1), F32),
                        pltpu.VMEM((N_HEADS, D_ATT), F32), pltpu.VMEM((N_HEADS, 1), F32)])
    out = pl.pallas_call(
        functools.partial(_attn_sample_kernel, pp=pp, page=page),
        grid_spec=grid_spec, out_shape=jax.ShapeDtypeStruct((bd, 1, D_ATT), BF16),
        compiler_params=_cparams(2), name="attn_sample",
    )(page_table, q[:, None, :], k_new[:, None, :], v_new[:, None, :], logf_new[:, :N_HEADS, None],
      *([cache_k] * pp), *([cache_v] * pp), *([cs_l] * pp))
    return out[:, 0, :]


def _layer_weights(l, tm, norm1_g, w_in, b_f, b_gate, q_gain, k_gain, pool_w, pool_scale, w_pool_up, w_att_up,
                   dw_w, dw_b, conv_ln_g, conv_ln_b, w_conv_out, w_out, norm2_g, w_mlp_up, w_mlp_down):
    d = w_in.shape[1]
    w = w_in[l]
    o_f = 3 * D_ATT
    o_pool = o_f + N_HEADS
    o_conv = o_pool + 256
    o_gate = o_conv + 512
    w_main = jnp.concatenate([w[:, :o_f], w[:, o_pool:o_gate], w[:, o_f:o_pool],
                              jnp.zeros((d, LANES - N_HEADS), F32)], axis=1).astype(BF16)
    grp = jnp.arange(256) // HEAD_DIM
    poolw = jnp.zeros((256, 256), F32)
    for g in range(4):
        poolw = poolw.at[g * 64:(g + 1) * 64, g * 64:(g + 1) * 64].set(pool_w[l, g])
    return dict(
        n1=norm1_g[l][None, :], w_main=w_main, w_gate=w[:, o_gate:].astype(BF16),
        bf=jnp.pad(b_f[l], (0, LANES - N_HEADS))[None, :], bg=b_gate[l],
        qg=jnp.tile(q_gain[l], N_HEADS)[None, :], kg=jnp.tile(k_gain[l], N_HEADS)[None, :],
        gsum=(grp[:, None] == grp[None, :]).astype(BF16),
        ltri=(jnp.arange(tm)[:, None] >= jnp.arange(tm)[None, :]).astype(BF16),
        poolw=poolw.astype(BF16), pscale=pool_scale[l][None, :],
        dww=jnp.pad(dw_w[l], ((0, CONV_HIST - CONV_WIDTH), (0, 0))), dwb=dw_b[l][None, :],
        lng=conv_ln_g[l][None, :], lnb=conv_ln_b[l][None, :],
        w_att_up=w_att_up[l].astype(BF16), w_pool_up=w_pool_up[l].astype(BF16),
        w_conv_out=w_conv_out[l].astype(BF16), w_out=w_out[l].astype(BF16),
        n2=norm2_g[l][None, :], w_up=w_mlp_up[l].astype(BF16), w_down=w_mlp_down[l].astype(BF16))


def kernel(x_prompt, x_sample, cache_k, cache_v, cache_logf, state_pool, state_conv, page_table,
           norm1_g, w_in, b_f, b_gate, q_gain, k_gain, pool_w, pool_scale, w_pool_up, w_att_up,
           dw_w, dw_b, conv_ln_g, conv_ln_b, w_conv_out, w_out, norm2_g, w_mlp_up, w_mlp_down):
    B, T, D = x_prompt.shape
    bd = x_sample.shape[0]
    depth = w_in.shape[0]
    n_pool_pages, page = cache_k.shape[1], cache_k.shape[2]
    n_pages = page_table.shape[1]
    past_len = n_pages * page
    tm = min(512, T)
    pp = min(32, n_pages)

    mc = (jnp.arange(page)[:, None] <= jnp.arange(page)[None, :]).astype(BF16)
    n_rows = n_pool_pages * N_HEADS
    tp = 4096 if n_rows % 4096 == 0 else n_rows
    ck_flat = jnp.transpose(cache_k, (0, 1, 3, 4, 2)).reshape(depth, n_pool_pages, D_ATT, page)
    cv_flat = jnp.transpose(cache_v, (0, 1, 3, 4, 2)).reshape(depth, n_pool_pages, D_ATT, page)
    logf_flat = jnp.transpose(cache_logf, (0, 1, 3, 2)).reshape(depth, n_rows, page)

    xp = x_prompt
    xs = x_sample.reshape(bd, D)
    outs = [[] for _ in range(7)]
    stacks = ()
    for l in range(depth):
        lw = _layer_weights(l, tm, norm1_g, w_in, b_f, b_gate, q_gain, k_gain, pool_w, pool_scale, w_pool_up,
                            w_att_up, dw_w, dw_b, conv_ln_g, conv_ln_b, w_conv_out, w_out, norm2_g,
                            w_mlp_up, w_mlp_down)
        qt, ka, vtb, kt_all, vt_all, ft_all, pm, ca, ptail, ctail = _inproj_prompt(xp, lw, tm, l, depth, stacks)
        stacks = (kt_all, vt_all, ft_all)
        att_t = _attn_prompt(qt, ka, vtb, tm)
        x1 = _merge(xp, att_t, pm, ca, lw, tm)
        xp = _mlp(x1.reshape(B * T, D), lw, tm).reshape(B, T, D)
        sp_t = jnp.transpose(state_pool[l], (1, 0, 2))
        sc_t = jnp.transpose(state_conv[l], (1, 0, 2))
        qs, ks, vs, fs, us, gs, pms, cas = _inproj_sample(xs, lw, sp_t, sc_t, past_len)
        cs_l = _page_cumsum(logf_flat, l, mc, tp)
        att_s = _attn_sample(page_table, qs, ks, vs, fs, ck_flat, cv_flat, l,
                             cs_l.reshape(n_pool_pages, N_HEADS, page), pp)
        xs1 = _merge(xs[None], att_s.T[None], pms[None], cas[None], lw, bd)[0]
        xs = _mlp(xs1, lw, bd)

        outs[0].append(ptail[:, POOL_HIST - POOL_BUF:])
        outs[1].append(ctail[:, CONV_HIST - CONV_BUF:])
        outs[2].append(ks.reshape(bd, 1, N_HEADS, HEAD_DIM))
        outs[3].append(vs.reshape(bd, 1, N_HEADS, HEAD_DIM))
        outs[4].append(fs[:, None, :N_HEADS])
        outs[5].append(jnp.concatenate([state_pool[l][:, 1:], us[:, None, :]], axis=1))
        outs[6].append(jnp.concatenate([state_conv[l][:, 1:], gs[:, None, :]], axis=1))
    kt_all, vt_all, ft_all = stacks
    k_prompt = jnp.transpose(kt_all.reshape(depth, B, N_HEADS, HEAD_DIM, T), (0, 1, 4, 2, 3))
    v_prompt = jnp.transpose(vt_all.reshape(depth, B, N_HEADS, HEAD_DIM, T), (0, 1, 4, 2, 3))
    f_prompt = jnp.transpose(ft_all, (0, 1, 3, 2))
    st = [jnp.stack(o) for o in outs]
    return (xp, xs.reshape(bd, 1, D), k_prompt, v_prompt, f_prompt) + tuple(st)
```

```python
import functools

import jax
import jax.numpy as jnp
from jax import lax
from jax.experimental import pallas as pl
from jax.experimental.pallas import tpu as pltpu

F32 = jnp.float32
BF16 = jnp.bfloat16

N_HEADS = 8
HEAD_DIM = 64
D_ATT = N_HEADS * HEAD_DIM
N_PAIRS = N_HEADS // 2
POOL_WINDOWS = (2, 4, 8, 16)
POOL_BUF = max(POOL_WINDOWS) - 1
POOL_HIST = 32
CONV_WIDTH = 31
CONV_BUF = CONV_WIDTH - 1
CONV_HIST = 32
NORM_EPS = 1e-6
LN_EPS = 1e-5
LOG2E = 1.4426950408889634
S_CHUNK = 512
LANES = 128
VMEM_LIMIT = 56 * 1024 * 1024

C_Q, C_K, C_V, C_POOL, C_CONV, C_F = 0, 512, 1024, 1536, 1792, 2304
D_MAIN = C_F + LANES


def _cparams(n_axes):
    return pltpu.CompilerParams(dimension_semantics=("arbitrary",) * n_axes, vmem_limit_bytes=VMEM_LIMIT)


def _split3(x):
    hi = x.astype(BF16).astype(F32)
    r = x - hi
    mid = r.astype(BF16).astype(F32)
    lo = (r - mid).astype(BF16).astype(F32)
    return hi, mid, lo


def _rms(x, g):
    return x * lax.rsqrt(jnp.mean(x * x, axis=-1, keepdims=True) + NORM_EPS) * g


def _log_sigmoid(x):
    return jnp.minimum(x, 0.0) - jnp.log1p(jnp.exp(-jnp.abs(x)))


def _head_norm(a, gsum, gain):
    outs = []
    for c in range(2):
        blk = a[:, c * 256:(c + 1) * 256]
        sq = blk * blk
        hi = sq.astype(BF16)
        lo = (sq - hi.astype(F32)).astype(BF16)
        ss = jnp.dot(hi, gsum, preferred_element_type=F32) + jnp.dot(lo, gsum, preferred_element_type=F32)
        outs.append(blk * lax.rsqrt(ss * (1.0 / HEAD_DIM) + NORM_EPS))
    return jnp.concatenate(outs, axis=1) * gain


def _pool_window_lanes(shape, col_block):
    lane = lax.broadcasted_iota(jnp.int32, shape, len(shape) - 1)
    lo_w, hi_w = POOL_WINDOWS[2 * col_block], POOL_WINDOWS[2 * col_block + 1]
    return jnp.where(lane < 64, lo_w, hi_w)


def _layer_norm_silu(y, g, b):
    mu = jnp.mean(y, axis=-1, keepdims=True)
    d = y - mu
    var = jnp.mean(d * d, axis=-1, keepdims=True)
    yn = d * lax.rsqrt(var + LN_EPS) * g + b
    return yn * jax.nn.sigmoid(yn)


def _inproj_prompt_kernel(x_ref, n1_ref, w_ref, bf_ref, qg_ref, kg_ref, gsum_ref, ltri_ref, poolw_ref,
                          pscale_ref, dww_ref, dwb_ref, lng_ref, lnb_ref,
                          *rest, tm, n_prev):
    (qt_ref, ka_ref, vtb_ref, kt_ref, vt_ref, logft_ref, pm_ref, ca_ref, ptail_ref, ctail_ref,
     uh_ref, gh_ref, sh_ref, carry_ref) = rest[n_prev:]
    t = pl.program_id(1)

    @pl.when(t == 0)
    def _():
        uh_ref[0:POOL_HIST, :] = jnp.zeros((POOL_HIST, 256), F32)
        gh_ref[0:CONV_HIST, :] = jnp.zeros((CONV_HIST, 256), F32)
        carry_ref[...] = jnp.zeros((1, LANES), F32)

    x = x_ref[0]
    h = _rms(x, n1_ref[...]).astype(BF16)
    z = jnp.dot(h, w_ref[...], preferred_element_type=F32)

    gsum = gsum_ref[...]
    qn = _head_norm(z[:, C_Q:C_Q + D_ATT], gsum, qg_ref[...]) * (HEAD_DIM ** -0.5 * LOG2E)
    kn = _head_norm(z[:, C_K:C_K + D_ATT], gsum, kg_ref[...])
    qt = qn.T
    vt = z[:, C_V:C_V + D_ATT].T
    kt_ref[0] = kn.T
    vt_ref[0] = vt

    lane = lax.broadcasted_iota(jnp.int32, (tm, LANES), 1)
    logf = jnp.where(lane < N_HEADS, _log_sigmoid(z[:, C_F:C_F + LANES] + bf_ref[...]), 0.0)
    logft_ref[0] = logf.T[0:N_HEADS, :]
    f_hi, f_mid, f_lo = _split3(logf)
    f3 = (f_hi + pltpu.roll(f_mid, 8, 1) + pltpu.roll(f_lo, 16, 1)).astype(BF16)
    cs3 = jnp.dot(ltri_ref[...], f3, preferred_element_type=F32)
    c = cs3 + pltpu.roll(cs3, LANES - 8, 1) + pltpu.roll(cs3, LANES - 16, 1) + carry_ref[...]
    c = jnp.where(lane < N_HEADS, c, 0.0)
    carry_ref[...] = c[tm - 1:tm, :]
    c2 = c * LOG2E
    c_hi, c_mid, c_lo = _split3(c2)
    ct_hi, ct_mid, ct_lo = _split3(c2.T[0:N_HEADS, :])

    row8 = lax.broadcasted_iota(jnp.int32, (8, tm), 0)
    pad_rows = jnp.zeros((HEAD_DIM - 8, tm), F32)
    for hd in range(N_HEADS):
        p, odd = hd // 2, hd % 2
        base = 0 if odd else 64
        ch, cm, cl = c_hi[:, hd:hd + 1], c_mid[:, hd:hd + 1], c_lo[:, hd:hd + 1]
        ones_k = (lane >= base) & (lane < base + 3)
        ek = jnp.where(lane == base + 3, -ch, jnp.where(lane == base + 4, -cm, jnp.where(lane == base + 5, -cl,
                       jnp.where(ones_k, 1.0, 0.0))))
        val = (lane >= 64) if odd else (lane < 64)
        ka_ref[0, hd] = jnp.where(val, kn[:, p * LANES:(p + 1) * LANES], ek).astype(BF16)
        eq = jnp.where(row8 == 0, ct_hi[hd:hd + 1, :], jnp.where(row8 == 1, ct_mid[hd:hd + 1, :],
                       jnp.where(row8 == 2, ct_lo[hd:hd + 1, :], jnp.where(row8 < 6, 1.0, 0.0))))
        qh = qt[hd * HEAD_DIM:(hd + 1) * HEAD_DIM, :]
        parts = [eq, pad_rows, qh] if odd else [qh, eq, pad_rows]
        qt_ref[0, hd] = jnp.concatenate(parts, axis=0).astype(BF16)
        vtb_ref[0, hd, 0] = vt[hd * HEAD_DIM:(hd + 1) * HEAD_DIM, :].astype(BF16)

    u = z[:, C_POOL:C_POOL + 256]
    uh_ref[POOL_HIST:POOL_HIST + tm, :] = u
    pos = t * tm + lax.broadcasted_iota(jnp.int32, (tm, LANES), 0)
    n_ext = POOL_HIST + tm
    mixed = []
    for cb in range(2):
        wl = _pool_window_lanes((tm, LANES), cb)
        e = uh_ref[:, cb * LANES:(cb + 1) * LANES]
        s2 = e[8:n_ext] + e[7:n_ext - 1]
        s4 = s2[8:] + s2[6:n_ext - 10]
        if cb == 0:
            lo, hi = s2[24:24 + tm], s4[16:16 + tm]
        else:
            s8 = s4[8:] + s4[4:n_ext - 20]
            s16 = s8[8:] + s8[0:n_ext - 32]
            lo, hi = s8[8:8 + tm], s16
        lane_b = lax.broadcasted_iota(jnp.int32, (tm, LANES), 1)
        acc = jnp.where(lane_b < 64, lo, hi)
        cnt = jnp.minimum(pos + 1, wl).astype(F32)
        mixed.append(acc / cnt - u[:, cb * LANES:(cb + 1) * LANES])
    mixed = jnp.concatenate(mixed, axis=1).astype(BF16)
    pm = jnp.dot(mixed, poolw_ref[...], preferred_element_type=F32) * pscale_ref[...]
    pm_ref[0] = pm.astype(BF16)
    ptail_ref[0] = uh_ref[tm:tm + POOL_HIST, :]
    uh_ref[0:POOL_HIST, :] = uh_ref[tm:tm + POOL_HIST, :]

    glu = z[:, C_CONV:C_CONV + 256] * jax.nn.sigmoid(z[:, C_CONV + 256:C_CONV + 512])
    gh_ref[CONV_HIST:CONV_HIST + tm, :] = glu
    y = jnp.zeros((tm, 256), F32) + dwb_ref[...]
    first = CONV_HIST - CONV_BUF
    for r in range(8):
        offs = [o for o in range(first, first + CONV_WIDTH) if o % 8 == r]
        a_max = max(offs) // 8
        if r:
            sh_ref[0:tm + 8 * a_max, :] = gh_ref[r:r + tm + 8 * a_max, :]
        src = sh_ref if r else gh_ref
        for o in offs:
            a = o // 8
            y = y + dww_ref[o - first:o - first + 1, :] * src[8 * a:8 * a + tm, :]
    ca_ref[0] = _layer_norm_silu(y, lng_ref[...], lnb_ref[...]).astype(BF16)
    ctail_ref[0] = gh_ref[tm:tm + CONV_HIST, :]
    gh_ref[0:CONV_HIST, :] = gh_ref[tm:tm + CONV_HIST, :]


def _inproj_prompt(x, lw, tm, layer, depth, prev):
    B, T, D = x.shape
    nt = T // tm
    const = lambda shape: pl.BlockSpec(shape, lambda b, t: (0,) * len(shape))
    row = lambda w: pl.BlockSpec((1, tm, w), lambda b, t: (b, t, 0))
    tail = lambda r: pl.BlockSpec((1, r, 256), lambda b, t: (b, 0, 0))
    stack = lambda r: pl.BlockSpec((None, 1, r, tm), lambda b, t: (layer, b, 0, t))
    out_shape = (
        jax.ShapeDtypeStruct((B, N_HEADS, LANES, T), BF16),
        jax.ShapeDtypeStruct((B, N_HEADS, T, LANES), BF16),
        jax.ShapeDtypeStruct((B, N_HEADS, nt, HEAD_DIM, tm), BF16),
        jax.ShapeDtypeStruct((depth, B, D_ATT, T), F32),
        jax.ShapeDtypeStruct((depth, B, D_ATT, T), F32),
        jax.ShapeDtypeStruct((depth, B, N_HEADS, T), F32),
        jax.ShapeDtypeStruct((B, T, 256), BF16),
        jax.ShapeDtypeStruct((B, T, 256), BF16),
        jax.ShapeDtypeStruct((B, POOL_HIST, 256), F32),
        jax.ShapeDtypeStruct((B, CONV_HIST, 256), F32),
    )
    out_specs = (pl.BlockSpec((1, N_HEADS, LANES, tm), lambda b, t: (b, 0, 0, t)),
                 pl.BlockSpec((1, N_HEADS, tm, LANES), lambda b, t: (b, 0, t, 0)),
                 pl.BlockSpec((1, N_HEADS, 1, HEAD_DIM, tm), lambda b, t: (b, 0, t, 0, 0)),
                 stack(D_ATT), stack(D_ATT), stack(N_HEADS), row(256), row(256), tail(POOL_HIST), tail(CONV_HIST))
    in_specs = [row(D), const((1, D)), const((D, D_MAIN)), const((1, LANES)), const((1, D_ATT)), const((1, D_ATT)),
                const((256, 256)), const((tm, tm)), const((256, 256)), const((1, 256)), const((CONV_HIST, 256)),
                const((1, 256)), const((1, 256)), const((1, 256))]
    n_in = len(in_specs)
    in_specs += [pl.BlockSpec(memory_space=pl.ANY)] * len(prev)
    aliases = {n_in + i: 3 + i for i in range(len(prev))}
    return pl.pallas_call(
        functools.partial(_inproj_prompt_kernel, tm=tm, n_prev=len(prev)),
        grid=(B, nt), in_specs=in_specs, out_specs=out_specs, out_shape=out_shape,
        scratch_shapes=[pltpu.VMEM((POOL_HIST + tm, 256), F32), pltpu.VMEM((CONV_HIST + tm, 256), F32),
                        pltpu.VMEM((CONV_HIST + tm, 256), F32), pltpu.VMEM((1, LANES), F32)],
        input_output_aliases=aliases,
        compiler_params=_cparams(2), name="inproj_prompt",
    )(x, lw["n1"], lw["w_main"], lw["bf"], lw["qg"], lw["kg"], lw["gsum"], lw["ltri"], lw["poolw"],
      lw["pscale"], lw["dww"], lw["dwb"], lw["lng"], lw["lnb"], *prev)


def _attn_prompt_kernel(qt_ref, ka_ref, vt_ref, o_ref, s_ref, p_ref, m_ref, l_ref, acc_ref, mx_ref, *, tq):
    i = pl.program_id(2)
    n_chunk = tq // S_CHUNK
    for hh in range(2):
        m_ref[hh] = jnp.full((1, tq), -jnp.inf, F32)
        l_ref[hh] = jnp.zeros((1, tq), F32)
        acc_ref[hh] = jnp.zeros((HEAD_DIM, tq), F32)

    def score(hh, j, masked):
        start = pl.multiple_of(j * tq, tq)
        for r in range(n_chunk):
            k = ka_ref[0, hh, pl.ds(start + r * S_CHUNK, S_CHUNK), :]
            st = jnp.dot(k, qt_ref[0, hh], preferred_element_type=F32)
            if masked:
                key = lax.broadcasted_iota(jnp.int32, (S_CHUNK, tq), 0) + r * S_CHUNK
                qry = lax.broadcasted_iota(jnp.int32, (S_CHUNK, tq), 1)
                st = jnp.where(key <= qry, st, -jnp.inf)
            s_ref[hh, r * S_CHUNK:(r + 1) * S_CHUNK, :] = st
            cm = jnp.max(st, axis=0, keepdims=True)
            mx = cm if r == 0 else jnp.maximum(mx, cm)
        mx_ref[hh] = mx

    def softmax(hh):
        m_old = m_ref[hh]
        m_new = jnp.maximum(m_old, mx_ref[hh])
        p = jnp.exp2(s_ref[hh] - m_new)
        alpha = jnp.exp2(m_old - m_new)
        l_ref[hh] = alpha * l_ref[hh] + jnp.sum(p, axis=0, keepdims=True)
        m_ref[hh] = m_new
        p_ref[hh] = p.astype(BF16)
        return alpha

    def pv(hh, jv, alpha):
        acc_ref[hh] = alpha * acc_ref[hh] + jnp.dot(vt_ref[0, hh, jv], p_ref[hh],
                                                   preferred_element_type=F32)

    for hh in range(2):
        score(hh, i, True)

    def body(t, carry):
        prev = jnp.where(t == 0, i, t - 1)
        for hh in range(2):
            alpha = softmax(hh)
            score(hh, t, False)
            pv(hh, prev, alpha)
        return carry

    lax.fori_loop(0, i, body, 0)
    last = jnp.maximum(i - 1, 0)
    for hh in range(2):
        pv(hh, last, softmax(hh))
    o = jnp.concatenate([acc_ref[0] / l_ref[0], acc_ref[1] / l_ref[1]], axis=0)
    o_ref[0] = o.astype(BF16)


def _attn_prompt(qt, ka, vtb, tq):
    B, _, _, T = qt.shape
    nq = T // tq
    return pl.pallas_call(
        functools.partial(_attn_prompt_kernel, tq=tq),
        grid=(B, N_PAIRS, nq),
        in_specs=[pl.BlockSpec((1, 2, LANES, tq), lambda b, p, i: (b, p, 0, i)),
                  pl.BlockSpec((1, 2, T, LANES), lambda b, p, i: (b, p, 0, 0)),
                  pl.BlockSpec((1, 2, nq, HEAD_DIM, tq), lambda b, p, i: (b, p, 0, 0, 0))],
        out_specs=pl.BlockSpec((1, 2 * HEAD_DIM, tq), lambda b, p, i: (b, p, i)),
        out_shape=jax.ShapeDtypeStruct((B, D_ATT, T), BF16),
        scratch_shapes=[pltpu.VMEM((2, tq, tq), F32), pltpu.VMEM((2, tq, tq), BF16),
                        pltpu.VMEM((2, 1, tq), F32), pltpu.VMEM((2, 1, tq), F32),
                        pltpu.VMEM((2, HEAD_DIM, tq), F32), pltpu.VMEM((2, 1, tq), F32)],
        compiler_params=_cparams(3), name="attn_prompt",
    )(qt, ka, vtb)


def _merge_kernel(x_ref, att_ref, pm_ref, ca_ref, n1_ref, wg_ref, bg_ref, wau_ref, wpu_ref, wco_ref, wo_ref, o_ref):
    x = x_ref[0]
    d = x.shape[1]
    h = _rms(x, n1_ref[...]).astype(BF16)
    branches = (lax.dot_general(att_ref[0], wau_ref[...], (((0,), (0,)), ((), ())), preferred_element_type=F32),
                jnp.dot(pm_ref[0], wpu_ref[...], preferred_element_type=F32),
                jnp.dot(ca_ref[0], wco_ref[...], preferred_element_type=F32))
    merged = jnp.zeros_like(x)
    for br in range(3):
        gl = jnp.dot(h, wg_ref[:, br * d:(br + 1) * d], preferred_element_type=F32) + bg_ref[br:br + 1, :]
        merged = merged + jax.nn.sigmoid(gl) * branches[br]
    o_ref[0] = x + jnp.dot(merged.astype(BF16), wo_ref[...], preferred_element_type=F32)


def _merge(x, att_t, pm, ca, lw, tm):
    B, T, d = x.shape
    const = lambda shape: pl.BlockSpec(shape, lambda b, t: (0, 0))
    row = lambda w: pl.BlockSpec((1, tm, w), lambda b, t: (b, t, 0))
    return pl.pallas_call(
        _merge_kernel, grid=(B, T // tm),
        in_specs=[row(d), pl.BlockSpec((1, D_ATT, tm), lambda b, t: (b, 0, t)), row(256), row(256),
                  const((1, d)), const((d, 3 * d)), const((3, d)),
                  const((D_ATT, d)), const((256, d)), const((256, d)), const((d, d))],
        out_specs=row(d), out_shape=jax.ShapeDtypeStruct((B, T, d), F32),
        compiler_params=_cparams(2), name="merge",
    )(x, att_t, pm, ca, lw["n1"], lw["w_gate"], lw["bg"], lw["w_att_up"], lw["w_pool_up"], lw["w_conv_out"], lw["w_out"])


def _mlp_kernel(x_ref, n2_ref, wu_ref, wd_ref, o_ref):
    x = x_ref[...]
    h = _rms(x, n2_ref[...]).astype(BF16)
    a = jnp.maximum(jnp.dot(h, wu_ref[...], preferred_element_type=F32), 0.0)
    hid = (a * a).astype(BF16)
    o_ref[...] = x + jnp.dot(hid, wd_ref[...], preferred_element_type=F32)


def _mlp(x, lw, tm):
    n, d = x.shape
    dff = lw["w_up"].shape[1]
    single = pl.Buffered(1)
    return pl.pallas_call(
        _mlp_kernel, grid=(n // tm,),
        in_specs=[pl.BlockSpec((tm, d), lambda t: (t, 0)), pl.BlockSpec((1, d), lambda t: (0, 0)),
                  pl.BlockSpec((d, dff), lambda t: (0, 0), pipeline_mode=single),
                  pl.BlockSpec((dff, d), lambda t: (0, 0), pipeline_mode=single)],
        out_specs=pl.BlockSpec((tm, d), lambda t: (t, 0)), out_shape=jax.ShapeDtypeStruct((n, d), F32),
        compiler_params=_cparams(1), name="mlp",
    )(x, lw["n2"], lw["w_up"], lw["w_down"])


def _inproj_sample_kernel(x_ref, n1_ref, w_ref, bf_ref, qg_ref, kg_ref, gsum_ref, poolw_ref, pscale_ref,
                          dww_ref, dwb_ref, lng_ref, lnb_ref, sp_ref, sc_ref,
                          q_ref, k_ref, v_ref, logf_ref, u_ref, glu_ref, pm_ref, ca_ref, *, start_pos):
    x = x_ref[...]
    n = x.shape[0]
    h = _rms(x, n1_ref[...]).astype(BF16)
    z = jnp.dot(h, w_ref[...], preferred_element_type=F32)
    gsum = gsum_ref[...]
    q_ref[...] = (_head_norm(z[:, C_Q:C_Q + D_ATT], gsum, qg_ref[...]) * (HEAD_DIM ** -0.5)).astype(BF16)
    k_ref[...] = _head_norm(z[:, C_K:C_K + D_ATT], gsum, kg_ref[...])
    v_ref[...] = z[:, C_V:C_V + D_ATT]
    logf_ref[...] = _log_sigmoid(z[:, C_F:C_F + LANES] + bf_ref[...])

    u = z[:, C_POOL:C_POOL + 256]
    u_ref[...] = u
    mixed = []
    for cb in range(2):
        wl = _pool_window_lanes((n, LANES), cb)
        acc = u[:, cb * LANES:(cb + 1) * LANES]
        for j in range(1, POOL_WINDOWS[2 * cb + 1]):
            acc = acc + jnp.where(wl > j, sp_ref[POOL_BUF - j, :, cb * LANES:(cb + 1) * LANES], 0.0)
        cnt = jnp.minimum(start_pos + 1, wl).astype(F32)
        mixed.append(acc / cnt - u[:, cb * LANES:(cb + 1) * LANES])
    mixed = jnp.concatenate(mixed, axis=1).astype(BF16)
    pm_ref[...] = (jnp.dot(mixed, poolw_ref[...], preferred_element_type=F32) * pscale_ref[...]).astype(BF16)

    glu = z[:, C_CONV:C_CONV + 256] * jax.nn.sigmoid(z[:, C_CONV + 256:C_CONV + 512])
    glu_ref[...] = glu
    y = dwb_ref[...] + dww_ref[CONV_BUF:CONV_BUF + 1, :] * glu
    for j in range(CONV_BUF):
        y = y + dww_ref[j:j + 1, :] * sc_ref[j]
    ca_ref[...] = _layer_norm_silu(y, lng_ref[...], lnb_ref[...]).astype(BF16)


def _inproj_sample(x, lw, sp_t, sc_t, start_pos):
    n, d = x.shape
    sds = jax.ShapeDtypeStruct
    out_shape = (sds((n, D_ATT), BF16), sds((n, D_ATT), F32), sds((n, D_ATT), F32), sds((n, LANES), F32),
                 sds((n, 256), F32), sds((n, 256), F32), sds((n, 256), BF16), sds((n, 256), BF16))
    return pl.pallas_call(
        functools.partial(_inproj_sample_kernel, start_pos=start_pos),
        out_shape=out_shape,
        compiler_params=pltpu.CompilerParams(vmem_limit_bytes=VMEM_LIMIT), name="inproj_sample",
    )(x, lw["n1"], lw["w_main"], lw["bf"], lw["qg"], lw["kg"], lw["gsum"], lw["poolw"], lw["pscale"],
      lw["dww"], lw["dwb"], lw["lng"], lw["lnb"], sp_t, sc_t)


def _page_cumsum_kernel(f_ref, mc_ref, o_ref):
    hi, mid, lo = _split3(f_ref[...])
    mc = mc_ref[...]
    o_ref[...] = (jnp.dot(hi.astype(BF16), mc, preferred_element_type=F32)
                  + jnp.dot(mid.astype(BF16), mc, preferred_element_type=F32)
                  + jnp.dot(lo.astype(BF16), mc, preferred_element_type=F32))


def _page_cumsum(logf_pages, layer, mc, tp):
    _, npg, w = logf_pages.shape
    return pl.pallas_call(
        _page_cumsum_kernel, grid=(npg // tp,),
        in_specs=[pl.BlockSpec((None, tp, w), lambda i: (layer, i, 0)), pl.BlockSpec((w, w), lambda i: (0, 0))],
        out_specs=pl.BlockSpec((tp, w), lambda i: (i, 0)), out_shape=jax.ShapeDtypeStruct((npg, w), F32),
        compiler_params=_cparams(1), name="page_cumsum",
    )(logf_pages, mc)


def _attn_sample_kernel(pt_ref, q_ref, kn_ref, vn_ref, fn_ref, *refs, pp, page):
    k_refs, v_refs, c_refs = refs[:pp], refs[pp:2 * pp], refs[2 * pp:3 * pp]
    o_ref = refs[3 * pp]
    m_ref, l_ref, acc_ref, carry_ref = refs[3 * pp + 1:]
    j = pl.program_id(1)
    nj = pl.num_programs(1)

    @pl.when(j == 0)
    def _():
        m_ref[...] = jnp.full((N_HEADS, 1), -jnp.inf, F32)
        l_ref[...] = jnp.zeros((N_HEADS, 1), F32)
        acc_ref[...] = jnp.zeros((N_HEADS, D_ATT), F32)
        carry_ref[...] = jnp.zeros((N_HEADS, 1), F32)

    head_of_lane = lax.broadcasted_iota(jnp.int32, (N_HEADS, D_ATT), 1) // HEAD_DIM
    own = head_of_lane == lax.broadcasted_iota(jnp.int32, (N_HEADS, D_ATT), 0)
    qbd32 = jnp.where(own, jnp.broadcast_to(q_ref[0].astype(F32), (N_HEADS, D_ATT)), 0.0)
    qbd = qbd32.astype(BF16)

    s_list = []
    carry = carry_ref[...]
    for i in range(pp):
        kt = k_refs[i][...].astype(BF16)
        s = jnp.dot(qbd, kt, preferred_element_type=F32)
        cs = c_refs[i][...]
        s_list.append(s - (cs + carry))
        carry = carry + cs[:, page - 1:page]
    carry_ref[...] = carry

    m_old = m_ref[...]
    m_blk = s_list[0]
    for s in s_list[1:]:
        m_blk = jnp.maximum(m_blk, s)
    m_new = jnp.maximum(m_old, jnp.max(m_blk, axis=1, keepdims=True))
    alpha = jnp.exp(m_old - m_new)
    p_sum = jnp.zeros((N_HEADS, page), F32)
    acc = alpha * acc_ref[...]
    for i in range(pp):
        p = jnp.exp(s_list[i] - m_new)
        p_sum = p_sum + p
        acc = acc + lax.dot_general(p.astype(BF16), v_refs[i][...].astype(BF16), (((1,), (1,)), ((), ())),
                                    preferred_element_type=F32)
    l_new = alpha * l_ref[...] + jnp.sum(p_sum, axis=1, keepdims=True)
    m_ref[...] = m_new
    l_ref[...] = l_new
    acc_ref[...] = acc

    @pl.when(j == nj - 1)
    def _():
        kn = jnp.broadcast_to(kn_ref[0], (N_HEADS, D_ATT)).astype(BF16).astype(F32)
        vn = jnp.broadcast_to(vn_ref[0], (N_HEADS, D_ATT)).astype(BF16).astype(F32)
        fn = fn_ref[0]
        s_new = jnp.sum(qbd32 * kn, axis=1, keepdims=True) - (carry + fn)
        m_fin = jnp.maximum(m_new, s_new)
        a_fin = jnp.exp(m_new - m_fin)
        p_new = jnp.exp(s_new - m_fin)
        l_fin = a_fin * l_new + p_new
        acc_fin = a_fin * acc + p_new.astype(BF16).astype(F32) * vn
        o = jnp.where(own, acc_fin / l_fin, 0.0)
        o_ref[0] = jnp.sum(o, axis=0, keepdims=True).astype(BF16)


def _attn_sample(page_table, q, k_new, v_new, logf_new, cache_k, cache_v, layer, cs_l, pp):
    bd, n_pages = page_table.shape
    page = cache_k.shape[3]
    nj = n_pages // pp
    tok = lambda w: pl.BlockSpec((1, 1, w), lambda b, j, pt: (b, 0, 0))
    kv_spec = lambda i: pl.BlockSpec((None, None, D_ATT, page),
                                     lambda b, j, pt, i=i: (layer, pt[b, j * pp + i], 0, 0))
    c_spec = lambda i: pl.BlockSpec((None, N_HEADS, page), lambda b, j, pt, i=i: (pt[b, j * pp + i], 0, 0))
    in_specs = ([tok(D_ATT), tok(D_ATT), tok(D_ATT), pl.BlockSpec((1, N_HEADS, 1), lambda b, j, pt: (b, 0, 0))]
                + [kv_spec(i) for i in range(pp)] + [kv_spec(i) for i in range(pp)] + [c_spec(i) for i in range(pp)])
    grid_spec = pltpu.PrefetchScalarGridSpec(
        num_scalar_prefetch=1, grid=(bd, nj), in_specs=in_specs,
        out_specs=pl.BlockSpec((1, 1, D_ATT), lambda b, j, pt: (b, 0, 0)),
        scratch_shapes=[pltpu.VMEM((N_HEADS, 1), F32), pltpu.VMEM((N_HEADS, 1), F32),
                        pltpu.VMEM((N_HEADS, D_ATT), F32), pltpu.VMEM((N_HEADS, 1), F32)])
    out = pl.pallas_call(
        functools.partial(_attn_sample_kernel, pp=pp, page=page),
        grid_spec=grid_spec, out_shape=jax.ShapeDtypeStruct((bd, 1, D_ATT), BF16),
        compiler_params=_cparams(2), name="attn_sample",
    )(page_table, q[:, None, :], k_new[:, None, :], v_new[:, None, :], logf_new[:, :N_HEADS, None],
      *([cache_k] * pp), *([cache_v] * pp), *([cs_l] * pp))
    return out[:, 0, :]


def _layer_weights(l, tm, norm1_g, w_in, b_f, b_gate, q_gain, k_gain, pool_w, pool_scale, w_pool_up, w_att_up,
                   dw_w, dw_b, conv_ln_g, conv_ln_b, w_conv_out, w_out, norm2_g, w_mlp_up, w_mlp_down):
    d = w_in.shape[1]
    w = w_in[l]
    o_f = 3 * D_ATT
    o_pool = o_f + N_HEADS
    o_conv = o_pool + 256
    o_gate = o_conv + 512
    w_main = jnp.concatenate([w[:, :o_f], w[:, o_pool:o_gate], w[:, o_f:o_pool],
                              jnp.zeros((d, LANES - N_HEADS), F32)], axis=1).astype(BF16)
    grp = jnp.arange(256) // HEAD_DIM
    poolw = jnp.zeros((256, 256), F32)
    for g in range(4):
        poolw = poolw.at[g * 64:(g + 1) * 64, g * 64:(g + 1) * 64].set(pool_w[l, g])
    return dict(
        n1=norm1_g[l][None, :], w_main=w_main, w_gate=w[:, o_gate:].astype(BF16),
        bf=jnp.pad(b_f[l], (0, LANES - N_HEADS))[None, :], bg=b_gate[l],
        qg=jnp.tile(q_gain[l], N_HEADS)[None, :], kg=jnp.tile(k_gain[l], N_HEADS)[None, :],
        gsum=(grp[:, None] == grp[None, :]).astype(BF16),
        ltri=(jnp.arange(tm)[:, None] >= jnp.arange(tm)[None, :]).astype(BF16),
        poolw=poolw.astype(BF16), pscale=pool_scale[l][None, :],
        dww=jnp.pad(dw_w[l], ((0, CONV_HIST - CONV_WIDTH), (0, 0))), dwb=dw_b[l][None, :],
        lng=conv_ln_g[l][None, :], lnb=conv_ln_b[l][None, :],
        w_att_up=w_att_up[l].astype(BF16), w_pool_up=w_pool_up[l].astype(BF16),
        w_conv_out=w_conv_out[l].astype(BF16), w_out=w_out[l].astype(BF16),
        n2=norm2_g[l][None, :], w_up=w_mlp_up[l].astype(BF16), w_down=w_mlp_down[l].astype(BF16))


def kernel(x_prompt, x_sample, cache_k, cache_v, cache_logf, state_pool, state_conv, page_table,
           norm1_g, w_in, b_f, b_gate, q_gain, k_gain, pool_w, pool_scale, w_pool_up, w_att_up,
           dw_w, dw_b, conv_ln_g, conv_ln_b, w_conv_out, w_out, norm2_g, w_mlp_up, w_mlp_down):
    B, T, D = x_prompt.shape
    bd = x_sample.shape[0]
    depth = w_in.shape[0]
    n_pool_pages, page = cache_k.shape[1], cache_k.shape[2]
    n_pages = page_table.shape[1]
    past_len = n_pages * page
    tm = min(512, T)
    pp = min(32, n_pages)

    mc = (jnp.arange(page)[:, None] <= jnp.arange(page)[None, :]).astype(BF16)
    n_rows = n_pool_pages * N_HEADS
    tp = 4096 if n_rows % 4096 == 0 else n_rows
    ck_flat = jnp.transpose(cache_k, (0, 1, 3, 4, 2)).reshape(depth, n_pool_pages, D_ATT, page)
    cv_flat = jnp.transpose(cache_v, (0, 1, 3, 4, 2)).reshape(depth, n_pool_pages, D_ATT, page)
    logf_flat = jnp.transpose(cache_logf, (0, 1, 3, 2)).reshape(depth, n_rows, page)

    xp = x_prompt
    xs = x_sample.reshape(bd, D)
    outs = [[] for _ in range(7)]
    stacks = ()
    for l in range(depth):
        lw = _layer_weights(l, tm, norm1_g, w_in, b_f, b_gate, q_gain, k_gain, pool_w, pool_scale, w_pool_up,
                            w_att_up, dw_w, dw_b, conv_ln_g, conv_ln_b, w_conv_out, w_out, norm2_g,
                            w_mlp_up, w_mlp_down)
        qt, ka, vtb, kt_all, vt_all, ft_all, pm, ca, ptail, ctail = _inproj_prompt(xp, lw, tm, l, depth, stacks)
        stacks = (kt_all, vt_all, ft_all)
        att_t = _attn_prompt(qt, ka, vtb, tm)
        x1 = _merge(xp, att_t, pm, ca, lw, tm)
        xp = _mlp(x1.reshape(B * T, D), lw, tm).reshape(B, T, D)
        sp_t = jnp.transpose(state_pool[l], (1, 0, 2))
        sc_t = jnp.transpose(state_conv[l], (1, 0, 2))
        qs, ks, vs, fs, us, gs, pms, cas = _inproj_sample(xs, lw, sp_t, sc_t, past_len)
        cs_l = _page_cumsum(logf_flat, l, mc, tp)
        att_s = _attn_sample(page_table, qs, ks, vs, fs, ck_flat, cv_flat, l,
                             cs_l.reshape(n_pool_pages, N_HEADS, page), pp)
        xs1 = _merge(xs[None], att_s.T[None], pms[None], cas[None], lw, bd)[0]
        xs = _mlp(xs1, lw, bd)

        outs[0].append(ptail[:, POOL_HIST - POOL_BUF:])
        outs[1].append(ctail[:, CONV_HIST - CONV_BUF:])
        outs[2].append(ks.reshape(bd, 1, N_HEADS, HEAD_DIM))
        outs[3].append(vs.reshape(bd, 1, N_HEADS, HEAD_DIM))
        outs[4].append(fs[:, None, :N_HEADS])
        outs[5].append(jnp.concatenate([state_pool[l][:, 1:], us[:, None, :]], axis=1))
        outs[6].append(jnp.concatenate([state_conv[l][:, 1:], gs[:, None, :]], axis=1))
    kt_all, vt_all, ft_all = stacks
    k_prompt = jnp.transpose(kt_all.reshape(depth, B, N_HEADS, HEAD_DIM, T), (0, 1, 4, 2, 3))
    v_prompt = jnp.transpose(vt_all.reshape(depth, B, N_HEADS, HEAD_DIM, T), (0, 1, 4, 2, 3))
    f_prompt = jnp.transpose(ft_all, (0, 1, 3, 2))
    st = [jnp.stack(o) for o in outs]
    return (xp, xs.reshape(bd, 1, D), k_prompt, v_prompt, f_prompt) + tuple(st)
```

```python
import functools

import jax
import jax.numpy as jnp
from jax import lax
from jax.experimental import pallas as pl
from jax.experimental.pallas import tpu as pltpu

F32 = jnp.float32
BF16 = jnp.bfloat16

N_HEADS = 8
HEAD_DIM = 64
D_ATT = N_HEADS * HEAD_DIM
N_PAIRS = N_HEADS // 2
POOL_WINDOWS = (2, 4, 8, 16)
POOL_BUF = max(POOL_WINDOWS) - 1
POOL_HIST = 32
CONV_WIDTH = 31
CONV_BUF = CONV_WIDTH - 1
CONV_HIST = 32
NORM_EPS = 1e-6
LN_EPS = 1e-5
LOG2E = 1.4426950408889634
N_SUB = 2
S_CHUNK = 512
LANES = 128
VMEM_LIMIT = 56 * 1024 * 1024

C_Q, C_K, C_V, C_POOL, C_CONV, C_F = 0, 512, 1024, 1536, 1792, 2304
D_MAIN = C_F + LANES


def _cparams(n_axes):
    return pltpu.CompilerParams(dimension_semantics=("arbitrary",) * n_axes, vmem_limit_bytes=VMEM_LIMIT)


def _split3(x):
    hi = x.astype(BF16).astype(F32)
    r = x - hi
    mid = r.astype(BF16).astype(F32)
    lo = (r - mid).astype(BF16).astype(F32)
    return hi, mid, lo


def _rms(x, g):
    return x * lax.rsqrt(jnp.mean(x * x, axis=-1, keepdims=True) + NORM_EPS) * g


def _log_sigmoid(x):
    return jnp.minimum(x, 0.0) - jnp.log1p(jnp.exp(-jnp.abs(x)))


def _head_norm(a, gsum, gain):
    outs = []
    for c in range(2):
        blk = a[:, c * 256:(c + 1) * 256]
        sq = blk * blk
        hi = sq.astype(BF16)
        lo = (sq - hi.astype(F32)).astype(BF16)
        ss = jnp.dot(hi, gsum, preferred_element_type=F32) + jnp.dot(lo, gsum, preferred_element_type=F32)
        outs.append(blk * lax.rsqrt(ss * (1.0 / HEAD_DIM) + NORM_EPS))
    return jnp.concatenate(outs, axis=1) * gain


def _pool_window_lanes(shape, col_block):
    lane = lax.broadcasted_iota(jnp.int32, shape, len(shape) - 1)
    lo_w, hi_w = POOL_WINDOWS[2 * col_block], POOL_WINDOWS[2 * col_block + 1]
    return jnp.where(lane < 64, lo_w, hi_w)


def _layer_norm_silu(y, g, b):
    mu = jnp.mean(y, axis=-1, keepdims=True)
    d = y - mu
    var = jnp.mean(d * d, axis=-1, keepdims=True)
    yn = d * lax.rsqrt(var + LN_EPS) * g + b
    return yn * jax.nn.sigmoid(yn)


def _inproj_prompt_kernel(x_ref, n1_ref, w_ref, bf_ref, qg_ref, kg_ref, gsum_ref, ltri_ref, poolw_ref,
                          pscale_ref, dww_ref, dwb_ref, lng_ref, lnb_ref,
                          *rest, tm, n_prev):
    (qt_ref, ka_ref, vtb_ref, kt_ref, vt_ref, logft_ref, pm_ref, ca_ref, ptail_ref, ctail_ref,
     uh_ref, gh_ref, sh_ref, carry_ref) = rest[n_prev:]
    t = pl.program_id(1)

    @pl.when(t == 0)
    def _():
        uh_ref[0:POOL_HIST, :] = jnp.zeros((POOL_HIST, 256), F32)
        gh_ref[0:CONV_HIST, :] = jnp.zeros((CONV_HIST, 256), F32)
        carry_ref[...] = jnp.zeros((1, LANES), F32)

    sub = tm // N_SUB
    rows = [slice(i * sub, (i + 1) * sub) for i in range(N_SUB)]
    zs = [jnp.dot(_rms(x_ref[0, rs, :], n1_ref[...]).astype(BF16), w_ref[...], preferred_element_type=F32)
          for rs in rows]
    gsum = gsum_ref[...]
    lane = lax.broadcasted_iota(jnp.int32, (sub, LANES), 1)
    row8 = lax.broadcasted_iota(jnp.int32, (8, sub), 0)
    pad_rows = jnp.zeros((HEAD_DIM - 8, sub), F32)
    first = CONV_HIST - CONV_BUF

    for rs, z in zip(rows, zs):
        r0 = rs.start
        qn = _head_norm(z[:, C_Q:C_Q + D_ATT], gsum, qg_ref[...]) * (HEAD_DIM ** -0.5 * LOG2E)
        kn = _head_norm(z[:, C_K:C_K + D_ATT], gsum, kg_ref[...])
        qt = qn.T
        vt = z[:, C_V:C_V + D_ATT].T
        kt_ref[0, :, rs] = kn.T
        vt_ref[0, :, rs] = vt

        logf = jnp.where(lane < N_HEADS, _log_sigmoid(z[:, C_F:C_F + LANES] + bf_ref[...]), 0.0)
        logft_ref[0, :, rs] = logf.T[0:N_HEADS, :]
        f_hi, f_mid, f_lo = _split3(logf)
        f3 = (f_hi + pltpu.roll(f_mid, 8, 1) + pltpu.roll(f_lo, 16, 1)).astype(BF16)
        cs3 = jnp.dot(ltri_ref[0:sub, 0:sub], f3, preferred_element_type=F32)
        c = cs3 + pltpu.roll(cs3, LANES - 8, 1) + pltpu.roll(cs3, LANES - 16, 1) + carry_ref[...]
        c = jnp.where(lane < N_HEADS, c, 0.0)
        carry_ref[...] = c[sub - 1:sub, :]
        c2 = c * LOG2E
        c_hi, c_mid, c_lo = _split3(c2)
        ct_hi, ct_mid, ct_lo = _split3(c2.T[0:N_HEADS, :])

        for hd in range(N_HEADS):
            p, odd = hd // 2, hd % 2
            base = 0 if odd else 64
            ch, cm, cl = c_hi[:, hd:hd + 1], c_mid[:, hd:hd + 1], c_lo[:, hd:hd + 1]
            ones_k = (lane >= base) & (lane < base + 3)
            ek = jnp.where(lane == base + 3, -ch, jnp.where(lane == base + 4, -cm, jnp.where(lane == base + 5, -cl,
                           jnp.where(ones_k, 1.0, 0.0))))
            val = (lane >= 64) if odd else (lane < 64)
            ka_ref[0, hd, rs, :] = jnp.where(val, kn[:, p * LANES:(p + 1) * LANES], ek).astype(BF16)
            eq = jnp.where(row8 == 0, ct_hi[hd:hd + 1, :], jnp.where(row8 == 1, ct_mid[hd:hd + 1, :],
                           jnp.where(row8 == 2, ct_lo[hd:hd + 1, :], jnp.where(row8 < 6, 1.0, 0.0))))
            qh = qt[hd * HEAD_DIM:(hd + 1) * HEAD_DIM, :]
            parts = [eq, pad_rows, qh] if odd else [qh, eq, pad_rows]
            qt_ref[0, hd, :, rs] = jnp.concatenate(parts, axis=0).astype(BF16)
            vtb_ref[0, hd, 0, :, rs] = vt[hd * HEAD_DIM:(hd + 1) * HEAD_DIM, :].astype(BF16)

        u = z[:, C_POOL:C_POOL + 256]
        uh_ref[POOL_HIST + r0:POOL_HIST + r0 + sub, :] = u
        pos = t * tm + r0 + lax.broadcasted_iota(jnp.int32, (sub, LANES), 0)
        n_ext = POOL_HIST + sub
        mixed = []
        for cb in range(2):
            wl = _pool_window_lanes((sub, LANES), cb)
            e = uh_ref[r0:r0 + n_ext, cb * LANES:(cb + 1) * LANES]
            s2 = e[8:n_ext] + e[7:n_ext - 1]
            s4 = s2[8:] + s2[6:n_ext - 10]
            if cb == 0:
                lo, hi = s2[24:24 + sub], s4[16:16 + sub]
            else:
                s8 = s4[8:] + s4[4:n_ext - 20]
                s16 = s8[8:] + s8[0:n_ext - 32]
                lo, hi = s8[8:8 + sub], s16
            acc = jnp.where(lane < 64, lo, hi)
            cnt = jnp.minimum(pos + 1, wl).astype(F32)
            mixed.append(acc / cnt - u[:, cb * LANES:(cb + 1) * LANES])
        mixed = jnp.concatenate(mixed, axis=1).astype(BF16)
        pm = jnp.dot(mixed, poolw_ref[...], preferred_element_type=F32) * pscale_ref[...]
        pm_ref[0, rs, :] = pm.astype(BF16)

        glu = z[:, C_CONV:C_CONV + 256] * jax.nn.sigmoid(z[:, C_CONV + 256:C_CONV + 512])
        gh_ref[CONV_HIST + r0:CONV_HIST + r0 + sub, :] = glu
        y = jnp.zeros((sub, 256), F32) + dwb_ref[...]
        for r in range(8):
            offs = [o for o in range(first, first + CONV_WIDTH) if o % 8 == r]
            a_max = max(offs) // 8
            if r:
                sh_ref[0:sub + 8 * a_max, :] = gh_ref[r0 + r:r0 + r + sub + 8 * a_max, :]
            for o in offs:
                a = o // 8
                tap = sh_ref[8 * a:8 * a + sub, :] if r else gh_ref[r0 + 8 * a:r0 + 8 * a + sub, :]
                y = y + dww_ref[o - first:o - first + 1, :] * tap
        ca_ref[0, rs, :] = _layer_norm_silu(y, lng_ref[...], lnb_ref[...]).astype(BF16)

    ptail_ref[0] = uh_ref[tm:tm + POOL_HIST, :]
    uh_ref[0:POOL_HIST, :] = uh_ref[tm:tm + POOL_HIST, :]
    ctail_ref[0] = gh_ref[tm:tm + CONV_HIST, :]
    gh_ref[0:CONV_HIST, :] = gh_ref[tm:tm + CONV_HIST, :]


def _inproj_prompt(x, lw, tm, layer, depth, prev):
    B, T, D = x.shape
    nt = T // tm
    const = lambda shape: pl.BlockSpec(shape, lambda b, t: (0,) * len(shape))
    row = lambda w: pl.BlockSpec((1, tm, w), lambda b, t: (b, t, 0))
    tail = lambda r: pl.BlockSpec((1, r, 256), lambda b, t: (b, 0, 0))
    stack = lambda r: pl.BlockSpec((None, 1, r, tm), lambda b, t: (layer, b, 0, t))
    out_shape = (
        jax.ShapeDtypeStruct((B, N_HEADS, LANES, T), BF16),
        jax.ShapeDtypeStruct((B, N_HEADS, T, LANES), BF16),
        jax.ShapeDtypeStruct((B, N_HEADS, nt, HEAD_DIM, tm), BF16),
        jax.ShapeDtypeStruct((depth, B, D_ATT, T), F32),
        jax.ShapeDtypeStruct((depth, B, D_ATT, T), F32),
        jax.ShapeDtypeStruct((depth, B, N_HEADS, T), F32),
        jax.ShapeDtypeStruct((B, T, 256), BF16),
        jax.ShapeDtypeStruct((B, T, 256), BF16),
        jax.ShapeDtypeStruct((B, POOL_HIST, 256), F32),
        jax.ShapeDtypeStruct((B, CONV_HIST, 256), F32),
    )
    out_specs = (pl.BlockSpec((1, N_HEADS, LANES, tm), lambda b, t: (b, 0, 0, t)),
                 pl.BlockSpec((1, N_HEADS, tm, LANES), lambda b, t: (b, 0, t, 0)),
                 pl.BlockSpec((1, N_HEADS, 1, HEAD_DIM, tm), lambda b, t: (b, 0, t, 0, 0)),
                 stack(D_ATT), stack(D_ATT), stack(N_HEADS), row(256), row(256), tail(POOL_HIST), tail(CONV_HIST))
    in_specs = [row(D), const((1, D)), const((D, D_MAIN)), const((1, LANES)), const((1, D_ATT)), const((1, D_ATT)),
                const((256, 256)), const((tm, tm)), const((256, 256)), const((1, 256)), const((CONV_HIST, 256)),
                const((1, 256)), const((1, 256)), const((1, 256))]
    n_in = len(in_specs)
    in_specs += [pl.BlockSpec(memory_space=pl.ANY)] * len(prev)
    aliases = {n_in + i: 3 + i for i in range(len(prev))}
    return pl.pallas_call(
        functools.partial(_inproj_prompt_kernel, tm=tm, n_prev=len(prev)),
        grid=(B, nt), in_specs=in_specs, out_specs=out_specs, out_shape=out_shape,
        scratch_shapes=[pltpu.VMEM((POOL_HIST + tm, 256), F32), pltpu.VMEM((CONV_HIST + tm, 256), F32),
                        pltpu.VMEM((CONV_HIST + tm, 256), F32), pltpu.VMEM((1, LANES), F32)],
        input_output_aliases=aliases,
        compiler_params=_cparams(2), name="inproj_prompt",
    )(x, lw["n1"], lw["w_main"], lw["bf"], lw["qg"], lw["kg"], lw["gsum"], lw["ltri"], lw["poolw"],
      lw["pscale"], lw["dww"], lw["dwb"], lw["lng"], lw["lnb"], *prev)


def _attn_prompt_kernel(qt_ref, ka_ref, vt_ref, o_ref, s_ref, p_ref, m_ref, l_ref, acc_ref, mx_ref, *, tq):
    i = pl.program_id(2)
    n_chunk = tq // S_CHUNK
    for hh in range(2):
        m_ref[hh] = jnp.full((1, tq), -jnp.inf, F32)
        l_ref[hh] = jnp.zeros((1, tq), F32)
        acc_ref[hh] = jnp.zeros((HEAD_DIM, tq), F32)

    def score(hh, j, masked):
        start = pl.multiple_of(j * tq, tq)
        for r in range(n_chunk):
            k = ka_ref[0, hh, pl.ds(start + r * S_CHUNK, S_CHUNK), :]
            st = jnp.dot(k, qt_ref[0, hh], preferred_element_type=F32)
            if masked:
                key = lax.broadcasted_iota(jnp.int32, (S_CHUNK, tq), 0) + r * S_CHUNK
                qry = lax.broadcasted_iota(jnp.int32, (S_CHUNK, tq), 1)
                st = jnp.where(key <= qry, st, -jnp.inf)
            s_ref[hh, r * S_CHUNK:(r + 1) * S_CHUNK, :] = st
            cm = jnp.max(st, axis=0, keepdims=True)
            mx = cm if r == 0 else jnp.maximum(mx, cm)
        mx_ref[hh] = mx

    def softmax(hh):
        m_old = m_ref[hh]
        m_new = jnp.maximum(m_old, mx_ref[hh])
        p = jnp.exp2(s_ref[hh] - m_new)
        alpha = jnp.exp2(m_old - m_new)
        l_ref[hh] = alpha * l_ref[hh] + jnp.sum(p, axis=0, keepdims=True)
        m_ref[hh] = m_new
        p_ref[hh] = p.astype(BF16)
        return alpha

    def pv(hh, jv, alpha):
        acc_ref[hh] = alpha * acc_ref[hh] + jnp.dot(vt_ref[0, hh, jv], p_ref[hh],
                                                   preferred_element_type=F32)

    for hh in range(2):
        score(hh, i, True)

    def body(t, carry):
        prev = jnp.where(t == 0, i, t - 1)
        for hh in range(2):
            alpha = softmax(hh)
            score(hh, t, False)
            pv(hh, prev, alpha)
        return carry

    lax.fori_loop(0, i, body, 0)
    last = jnp.maximum(i - 1, 0)
    for hh in range(2):
        pv(hh, last, softmax(hh))
    o = jnp.concatenate([acc_ref[0] / l_ref[0], acc_ref[1] / l_ref[1]], axis=0)
    o_ref[0] = o.astype(BF16)


def _attn_prompt(qt, ka, vtb, tq):
    B, _, _, T = qt.shape
    nq = T // tq
    return pl.pallas_call(
        functools.partial(_attn_prompt_kernel, tq=tq),
        grid=(B, N_PAIRS, nq),
        in_specs=[pl.BlockSpec((1, 2, LANES, tq), lambda b, p, i: (b, p, 0, i)),
                  pl.BlockSpec((1, 2, T, LANES), lambda b, p, i: (b, p, 0, 0)),
                  pl.BlockSpec((1, 2, nq, HEAD_DIM, tq), lambda b, p, i: (b, p, 0, 0, 0))],
        out_specs=pl.BlockSpec((1, 2 * HEAD_DIM, tq), lambda b, p, i: (b, p, i)),
        out_shape=jax.ShapeDtypeStruct((B, D_ATT, T), BF16),
        scratch_shapes=[pltpu.VMEM((2, tq, tq), F32), pltpu.VMEM((2, tq, tq), BF16),
                        pltpu.VMEM((2, 1, tq), F32), pltpu.VMEM((2, 1, tq), F32),
                        pltpu.VMEM((2, HEAD_DIM, tq), F32), pltpu.VMEM((2, 1, tq), F32)],
        compiler_params=_cparams(3), name="attn_prompt",
    )(qt, ka, vtb)


def _merge_kernel(x_ref, att_ref, pm_ref, ca_ref, n1_ref, wg_ref, bg_ref, wau_ref, wpu_ref, wco_ref, wo_ref, o_ref):
    x = x_ref[0]
    d = x.shape[1]
    h = _rms(x, n1_ref[...]).astype(BF16)
    branches = (lax.dot_general(att_ref[0], wau_ref[...], (((0,), (0,)), ((), ())), preferred_element_type=F32),
                jnp.dot(pm_ref[0], wpu_ref[...], preferred_element_type=F32),
                jnp.dot(ca_ref[0], wco_ref[...], preferred_element_type=F32))
    merged = jnp.zeros_like(x)
    for br in range(3):
        gl = jnp.dot(h, wg_ref[:, br * d:(br + 1) * d], preferred_element_type=F32) + bg_ref[br:br + 1, :]
        merged = merged + jax.nn.sigmoid(gl) * branches[br]
    o_ref[0] = x + jnp.dot(merged.astype(BF16), wo_ref[...], preferred_element_type=F32)


def _merge(x, att_t, pm, ca, lw, tm):
    B, T, d = x.shape
    const = lambda shape: pl.BlockSpec(shape, lambda b, t: (0, 0))
    row = lambda w: pl.BlockSpec((1, tm, w), lambda b, t: (b, t, 0))
    return pl.pallas_call(
        _merge_kernel, grid=(B, T // tm),
        in_specs=[row(d), pl.BlockSpec((1, D_ATT, tm), lambda b, t: (b, 0, t)), row(256), row(256),
                  const((1, d)), const((d, 3 * d)), const((3, d)),
                  const((D_ATT, d)), const((256, d)), const((256, d)), const((d, d))],
        out_specs=row(d), out_shape=jax.ShapeDtypeStruct((B, T, d), F32),
        compiler_params=_cparams(2), name="merge",
    )(x, att_t, pm, ca, lw["n1"], lw["w_gate"], lw["bg"], lw["w_att_up"], lw["w_pool_up"], lw["w_conv_out"], lw["w_out"])


def _mlp_kernel(x_ref, n2_ref, wu_ref, wd_ref, o_ref):
    x = x_ref[...]
    h = _rms(x, n2_ref[...]).astype(BF16)
    a = jnp.maximum(jnp.dot(h, wu_ref[...], preferred_element_type=F32), 0.0)
    hid = (a * a).astype(BF16)
    o_ref[...] = x + jnp.dot(hid, wd_ref[...], preferred_element_type=F32)


def _mlp(x, lw, tm):
    n, d = x.shape
    dff = lw["w_up"].shape[1]
    single = pl.Buffered(1)
    return pl.pallas_call(
        _mlp_kernel, grid=(n // tm,),
        in_specs=[pl.BlockSpec((tm, d), lambda t: (t, 0)), pl.BlockSpec((1, d), lambda t: (0, 0)),
                  pl.BlockSpec((d, dff), lambda t: (0, 0), pipeline_mode=single),
                  pl.BlockSpec((dff, d), lambda t: (0, 0), pipeline_mode=single)],
        out_specs=pl.BlockSpec((tm, d), lambda t: (t, 0)), out_shape=jax.ShapeDtypeStruct((n, d), F32),
        compiler_params=_cparams(1), name="mlp",
    )(x, lw["n2"], lw["w_up"], lw["w_down"])


def _inproj_sample_kernel(x_ref, n1_ref, w_ref, bf_ref, qg_ref, kg_ref, gsum_ref, poolw_ref, pscale_ref,
                          dww_ref, dwb_ref, lng_ref, lnb_ref, sp_ref, sc_ref,
                          q_ref, k_ref, v_ref, logf_ref, u_ref, glu_ref, pm_ref, ca_ref, *, start_pos):
    x = x_ref[...]
    n = x.shape[0]
    h = _rms(x, n1_ref[...]).astype(BF16)
    z = jnp.dot(h, w_ref[...], preferred_element_type=F32)
    gsum = gsum_ref[...]
    q_ref[...] = (_head_norm(z[:, C_Q:C_Q + D_ATT], gsum, qg_ref[...]) * (HEAD_DIM ** -0.5)).astype(BF16)
    k_ref[...] = _head_norm(z[:, C_K:C_K + D_ATT], gsum, kg_ref[...])
    v_ref[...] = z[:, C_V:C_V + D_ATT]
    logf_ref[...] = _log_sigmoid(z[:, C_F:C_F + LANES] + bf_ref[...])

    u = z[:, C_POOL:C_POOL + 256]
    u_ref[...] = u
    mixed = []
    for cb in range(2):
        wl = _pool_window_lanes((n, LANES), cb)
        acc = u[:, cb * LANES:(cb + 1) * LANES]
        for j in range(1, POOL_WINDOWS[2 * cb + 1]):
            acc = acc + jnp.where(wl > j, sp_ref[POOL_BUF - j, :, cb * LANES:(cb + 1) * LANES], 0.0)
        cnt = jnp.minimum(start_pos + 1, wl).astype(F32)
        mixed.append(acc / cnt - u[:, cb * LANES:(cb + 1) * LANES])
    mixed = jnp.concatenate(mixed, axis=1).astype(BF16)
    pm_ref[...] = (jnp.dot(mixed, poolw_ref[...], preferred_element_type=F32) * pscale_ref[...]).astype(BF16)

    glu = z[:, C_CONV:C_CONV + 256] * jax.nn.sigmoid(z[:, C_CONV + 256:C_CONV + 512])
    glu_ref[...] = glu
    y = dwb_ref[...] + dww_ref[CONV_BUF:CONV_BUF + 1, :] * glu
    for j in range(CONV_BUF):
        y = y + dww_ref[j:j + 1, :] * sc_ref[j]
    ca_ref[...] = _layer_norm_silu(y, lng_ref[...], lnb_ref[...]).astype(BF16)


def _inproj_sample(x, lw, sp_t, sc_t, start_pos):
    n, d = x.shape
    sds = jax.ShapeDtypeStruct
    out_shape = (sds((n, D_ATT), BF16), sds((n, D_ATT), F32), sds((n, D_ATT), F32), sds((n, LANES), F32),
                 sds((n, 256), F32), sds((n, 256), F32), sds((n, 256), BF16), sds((n, 256), BF16))
    return pl.pallas_call(
        functools.partial(_inproj_sample_kernel, start_pos=start_pos),
        out_shape=out_shape,
        compiler_params=pltpu.CompilerParams(vmem_limit_bytes=VMEM_LIMIT), name="inproj_sample",
    )(x, lw["n1"], lw["w_main"], lw["bf"], lw["qg"], lw["kg"], lw["gsum"], lw["poolw"], lw["pscale"],
      lw["dww"], lw["dwb"], lw["lng"], lw["lnb"], sp_t, sc_t)


def _attn_sample_kernel(pt_ref, q_ref, kn_ref, vn_ref, fn_ref, tri_ref, *refs, pp, page):
    k_refs, v_refs, f_refs = refs[:pp], refs[pp:2 * pp], refs[2 * pp:3 * pp]
    o_ref = refs[3 * pp]
    m_ref, l_ref, acc_ref, carry_ref = refs[3 * pp + 1:]
    j = pl.program_id(1)
    nj = pl.num_programs(1)

    @pl.when(j == 0)
    def _():
        m_ref[...] = jnp.full((N_HEADS, 1), -jnp.inf, F32)
        l_ref[...] = jnp.zeros((N_HEADS, 1), F32)
        acc_ref[...] = jnp.zeros((N_HEADS, D_ATT), F32)
        carry_ref[...] = jnp.zeros((N_HEADS, 1), F32)

    head_of_lane = lax.broadcasted_iota(jnp.int32, (N_HEADS, D_ATT), 1) // HEAD_DIM
    own = head_of_lane == lax.broadcasted_iota(jnp.int32, (N_HEADS, D_ATT), 0)
    qbd32 = jnp.where(own, jnp.broadcast_to(q_ref[0].astype(F32), (N_HEADS, D_ATT)), 0.0)
    qbd = qbd32.astype(BF16)

    f_hi, f_mid, f_lo = _split3(jnp.concatenate([f_refs[i][...] for i in range(pp)], axis=0))
    tri = tri_ref[...]
    cs_all = (jnp.dot(f_hi.astype(BF16), tri, preferred_element_type=F32)
              + jnp.dot(f_mid.astype(BF16), tri, preferred_element_type=F32)
              + jnp.dot(f_lo.astype(BF16), tri, preferred_element_type=F32))

    s_list = []
    carry = carry_ref[...]
    for i in range(pp):
        kt = k_refs[i][...].astype(BF16)
        s = jnp.dot(qbd, kt, preferred_element_type=F32)
        cs = cs_all[i * N_HEADS:(i + 1) * N_HEADS, :]
        s_list.append(s - (cs + carry))
        carry = carry + cs[:, page - 1:page]
    carry_ref[...] = carry

    m_old = m_ref[...]
    m_blk = s_list[0]
    for s in s_list[1:]:
        m_blk = jnp.maximum(m_blk, s)
    m_new = jnp.maximum(m_old, jnp.max(m_blk, axis=1, keepdims=True))
    alpha = jnp.exp(m_old - m_new)
    p_sum = jnp.zeros((N_HEADS, page), F32)
    acc = alpha * acc_ref[...]
    for i in range(pp):
        p = jnp.exp(s_list[i] - m_new)
        p_sum = p_sum + p
        acc = acc + lax.dot_general(p.astype(BF16), v_refs[i][...].astype(BF16), (((1,), (1,)), ((), ())),
                                    preferred_element_type=F32)
    l_new = alpha * l_ref[...] + jnp.sum(p_sum, axis=1, keepdims=True)
    m_ref[...] = m_new
    l_ref[...] = l_new
    acc_ref[...] = acc

    @pl.when(j == nj - 1)
    def _():
        kn = jnp.broadcast_to(kn_ref[0], (N_HEADS, D_ATT)).astype(BF16).astype(F32)
        vn = jnp.broadcast_to(vn_ref[0], (N_HEADS, D_ATT)).astype(BF16).astype(F32)
        fn = fn_ref[0]
        s_new = jnp.sum(qbd32 * kn, axis=1, keepdims=True) - (carry + fn)
        m_fin = jnp.maximum(m_new, s_new)
        a_fin = jnp.exp(m_new - m_fin)
        p_new = jnp.exp(s_new - m_fin)
        l_fin = a_fin * l_new + p_new
        acc_fin = a_fin * acc + p_new.astype(BF16).astype(F32) * vn
        o = jnp.where(own, acc_fin / l_fin, 0.0)
        o_ref[0] = jnp.sum(o, axis=0, keepdims=True).astype(BF16)


def _attn_sample(page_table, q, k_new, v_new, logf_new, tri, cache_k, cache_v, cache_f, layer, pp):
    bd, n_pages = page_table.shape
    page = cache_k.shape[3]
    nj = n_pages // pp
    tok = lambda w: pl.BlockSpec((1, 1, w), lambda b, j, pt: (b, 0, 0))
    kv_spec = lambda i: pl.BlockSpec((None, None, D_ATT, page),
                                     lambda b, j, pt, i=i: (layer, pt[b, j * pp + i], 0, 0))
    f_spec = lambda i: pl.BlockSpec((None, None, N_HEADS, page),
                                    lambda b, j, pt, i=i: (layer, pt[b, j * pp + i], 0, 0))
    in_specs = ([tok(D_ATT), tok(D_ATT), tok(D_ATT), pl.BlockSpec((1, N_HEADS, 1), lambda b, j, pt: (b, 0, 0)),
                 pl.BlockSpec((page, page), lambda b, j, pt: (0, 0))]
                + [kv_spec(i) for i in range(pp)] + [kv_spec(i) for i in range(pp)] + [f_spec(i) for i in range(pp)])
    grid_spec = pltpu.PrefetchScalarGridSpec(
        num_scalar_prefetch=1, grid=(bd, nj), in_specs=in_specs,
        out_specs=pl.BlockSpec((1, 1, D_ATT), lambda b, j, pt: (b, 0, 0)),
        scratch_shapes=[pltpu.VMEM((N_HEADS, 1), F32), pltpu.VMEM((N_HEADS, 1), F32),
                        pltpu.VMEM((N_HEADS, D_ATT), F32), pltpu.VMEM((N_HEADS, 1), F32)])
    out = pl.pallas_call(
        functools.partial(_attn_sample_kernel, pp=pp, page=page),
        grid_spec=grid_spec, out_shape=jax.ShapeDtypeStruct((bd, 1, D_ATT), BF16),
        compiler_params=_cparams(2), name="attn_sample",
    )(page_table, q[:, None, :], k_new[:, None, :], v_new[:, None, :], logf_new[:, :N_HEADS, None], tri,
      *([cache_k] * pp), *([cache_v] * pp), *([cache_f] * pp))
    return out[:, 0, :]


def _layer_weights(l, tm, norm1_g, w_in, b_f, b_gate, q_gain, k_gain, pool_w, pool_scale, w_pool_up, w_att_up,
                   dw_w, dw_b, conv_ln_g, conv_ln_b, w_conv_out, w_out, norm2_g, w_mlp_up, w_mlp_down):
    d = w_in.shape[1]
    w = w_in[l]
    o_f = 3 * D_ATT
    o_pool = o_f + N_HEADS
    o_conv = o_pool + 256
    o_gate = o_conv + 512
    w_main = jnp.concatenate([w[:, :o_f], w[:, o_pool:o_gate], w[:, o_f:o_pool],
                              jnp.zeros((d, LANES - N_HEADS), F32)], axis=1).astype(BF16)
    grp = jnp.arange(256) // HEAD_DIM
    poolw = jnp.zeros((256, 256), F32)
    for g in range(4):
        poolw = poolw.at[g * 64:(g + 1) * 64, g * 64:(g + 1) * 64].set(pool_w[l, g])
    return dict(
        n1=norm1_g[l][None, :], w_main=w_main, w_gate=w[:, o_gate:].astype(BF16),
        bf=jnp.pad(b_f[l], (0, LANES - N_HEADS))[None, :], bg=b_gate[l],
        qg=jnp.tile(q_gain[l], N_HEADS)[None, :], kg=jnp.tile(k_gain[l], N_HEADS)[None, :],
        gsum=(grp[:, None] == grp[None, :]).astype(BF16),
        ltri=(jnp.arange(tm)[:, None] >= jnp.arange(tm)[None, :]).astype(BF16),
        poolw=poolw.astype(BF16), pscale=pool_scale[l][None, :],
        dww=jnp.pad(dw_w[l], ((0, CONV_HIST - CONV_WIDTH), (0, 0))), dwb=dw_b[l][None, :],
        lng=conv_ln_g[l][None, :], lnb=conv_ln_b[l][None, :],
        w_att_up=w_att_up[l].astype(BF16), w_pool_up=w_pool_up[l].astype(BF16),
        w_conv_out=w_conv_out[l].astype(BF16), w_out=w_out[l].astype(BF16),
        n2=norm2_g[l][None, :], w_up=w_mlp_up[l].astype(BF16), w_down=w_mlp_down[l].astype(BF16))


def kernel(x_prompt, x_sample, cache_k, cache_v, cache_logf, state_pool, state_conv, page_table,
           norm1_g, w_in, b_f, b_gate, q_gain, k_gain, pool_w, pool_scale, w_pool_up, w_att_up,
           dw_w, dw_b, conv_ln_g, conv_ln_b, w_conv_out, w_out, norm2_g, w_mlp_up, w_mlp_down):
    B, T, D = x_prompt.shape
    bd = x_sample.shape[0]
    depth = w_in.shape[0]
    n_pool_pages, page = cache_k.shape[1], cache_k.shape[2]
    n_pages = page_table.shape[1]
    past_len = n_pages * page
    tm = min(512, T)
    pp = min(32, n_pages)

    tri = (jnp.arange(page)[:, None] <= jnp.arange(page)[None, :]).astype(BF16)
    ck_flat = jnp.transpose(cache_k, (0, 1, 3, 4, 2)).reshape(depth, n_pool_pages, D_ATT, page)
    cv_flat = jnp.transpose(cache_v, (0, 1, 3, 4, 2)).reshape(depth, n_pool_pages, D_ATT, page)
    cf_flat = jnp.transpose(cache_logf, (0, 1, 3, 2))

    xp = x_prompt
    xs = x_sample.reshape(bd, D)
    outs = [[] for _ in range(7)]
    stacks = ()
    for l in range(depth):
        lw = _layer_weights(l, tm, norm1_g, w_in, b_f, b_gate, q_gain, k_gain, pool_w, pool_scale, w_pool_up,
                            w_att_up, dw_w, dw_b, conv_ln_g, conv_ln_b, w_conv_out, w_out, norm2_g,
                            w_mlp_up, w_mlp_down)
        qt, ka, vtb, kt_all, vt_all, ft_all, pm, ca, ptail, ctail = _inproj_prompt(xp, lw, tm, l, depth, stacks)
        stacks = (kt_all, vt_all, ft_all)
        att_t = _attn_prompt(qt, ka, vtb, tm)
        x1 = _merge(xp, att_t, pm, ca, lw, tm)
        xp = _mlp(x1.reshape(B * T, D), lw, tm).reshape(B, T, D)
        sp_t = jnp.transpose(state_pool[l], (1, 0, 2))
        sc_t = jnp.transpose(state_conv[l], (1, 0, 2))
        qs, ks, vs, fs, us, gs, pms, cas = _inproj_sample(xs, lw, sp_t, sc_t, past_len)
        att_s = _attn_sample(page_table, qs, ks, vs, fs, tri, ck_flat, cv_flat, cf_flat, l, pp)
        xs1 = _merge(xs[None], att_s.T[None], pms[None], cas[None], lw, bd)[0]
        xs = _mlp(xs1, lw, bd)

        outs[0].append(ptail[:, POOL_HIST - POOL_BUF:])
        outs[1].append(ctail[:, CONV_HIST - CONV_BUF:])
        outs[2].append(ks.reshape(bd, 1, N_HEADS, HEAD_DIM))
        outs[3].append(vs.reshape(bd, 1, N_HEADS, HEAD_DIM))
        outs[4].append(fs[:, None, :N_HEADS])
        outs[5].append(jnp.concatenate([state_pool[l][:, 1:], us[:, None, :]], axis=1))
        outs[6].append(jnp.concatenate([state_conv[l][:, 1:], gs[:, None, :]], axis=1))
    kt_all, vt_all, ft_all = stacks
    k_prompt = jnp.transpose(kt_all.reshape(depth, B, N_HEADS, HEAD_DIM, T), (0, 1, 4, 2, 3))
    v_prompt = jnp.transpose(vt_all.reshape(depth, B, N_HEADS, HEAD_DIM, T), (0, 1, 4, 2, 3))
    f_prompt = jnp.transpose(ft_all, (0, 1, 3, 2))
    st = [jnp.stack(o) for o in outs]
    return (xp, xs.reshape(bd, 1, D), k_prompt, v_prompt, f_prompt) + tuple(st)
```

```python
import functools

import jax
import jax.numpy as jnp
from jax import lax
from jax.experimental import pallas as pl
from jax.experimental.pallas import tpu as pltpu

F32 = jnp.float32
BF16 = jnp.bfloat16

N_HEADS = 8
HEAD_DIM = 64
D_ATT = N_HEADS * HEAD_DIM
ATT_HEADS = 4
POOL_WINDOWS = (2, 4, 8, 16)
POOL_BUF = max(POOL_WINDOWS) - 1
POOL_HIST = 32
CONV_WIDTH = 31
CONV_BUF = CONV_WIDTH - 1
CONV_HIST = 32
NORM_EPS = 1e-6
LN_EPS = 1e-5
LOG2E = 1.4426950408889634
N_SUB = 2
S_CHUNK = 512
LANES = 128
VMEM_LIMIT = 56 * 1024 * 1024

C_Q, C_K, C_V, C_POOL, C_CONV, C_F = 0, 512, 1024, 1536, 1792, 2304
D_MAIN = C_F + LANES


def _cparams(n_axes):
    return pltpu.CompilerParams(dimension_semantics=("arbitrary",) * n_axes, vmem_limit_bytes=VMEM_LIMIT)


def _split3(x):
    hi = x.astype(BF16).astype(F32)
    r = x - hi
    mid = r.astype(BF16).astype(F32)
    lo = (r - mid).astype(BF16).astype(F32)
    return hi, mid, lo


def _rms(x, g):
    return x * lax.rsqrt(jnp.mean(x * x, axis=-1, keepdims=True) + NORM_EPS) * g


def _log_sigmoid(x):
    return jnp.minimum(x, 0.0) - jnp.log1p(jnp.exp(-jnp.abs(x)))


def _head_norm(a, gsum, gain):
    outs = []
    for c in range(2):
        blk = a[:, c * 256:(c + 1) * 256]
        sq = blk * blk
        hi = sq.astype(BF16)
        lo = (sq - hi.astype(F32)).astype(BF16)
        ss = jnp.dot(hi, gsum, preferred_element_type=F32) + jnp.dot(lo, gsum, preferred_element_type=F32)
        outs.append(blk * lax.rsqrt(ss * (1.0 / HEAD_DIM) + NORM_EPS))
    return jnp.concatenate(outs, axis=1) * gain


def _pool_window_lanes(shape, col_block):
    lane = lax.broadcasted_iota(jnp.int32, shape, len(shape) - 1)
    lo_w, hi_w = POOL_WINDOWS[2 * col_block], POOL_WINDOWS[2 * col_block + 1]
    return jnp.where(lane < 64, lo_w, hi_w)


def _layer_norm_silu(y, g, b):
    mu = jnp.mean(y, axis=-1, keepdims=True)
    d = y - mu
    var = jnp.mean(d * d, axis=-1, keepdims=True)
    yn = d * lax.rsqrt(var + LN_EPS) * g + b
    return yn * jax.nn.sigmoid(yn)


def _inproj_prompt_kernel(x_ref, n1_ref, w_ref, bf_ref, qg_ref, kg_ref, gsum_ref, ltri_ref, poolw_ref,
                          pscale_ref, dww_ref, dwb_ref, lng_ref, lnb_ref,
                          *rest, tm, n_prev):
    (qt_ref, ka_ref, vtb_ref, kt_ref, vt_ref, logft_ref, pm_ref, ca_ref, ptail_ref, ctail_ref,
     uh_ref, gh_ref, sh_ref, carry_ref) = rest[n_prev:]
    t = pl.program_id(1)

    @pl.when(t == 0)
    def _():
        uh_ref[0:POOL_HIST, :] = jnp.zeros((POOL_HIST, 256), F32)
        gh_ref[0:CONV_HIST, :] = jnp.zeros((CONV_HIST, 256), F32)
        carry_ref[...] = jnp.zeros((1, LANES), F32)

    sub = tm // N_SUB
    rows = [slice(i * sub, (i + 1) * sub) for i in range(N_SUB)]
    zs = [jnp.dot(_rms(x_ref[0, rs, :], n1_ref[...]).astype(BF16), w_ref[...], preferred_element_type=F32)
          for rs in rows]
    gsum = gsum_ref[...]
    lane = lax.broadcasted_iota(jnp.int32, (sub, LANES), 1)
    row8 = lax.broadcasted_iota(jnp.int32, (8, sub), 0)
    pad_rows = jnp.zeros((HEAD_DIM - 8, sub), F32)
    first = CONV_HIST - CONV_BUF

    for rs, z in zip(rows, zs):
        r0 = rs.start
        qn = _head_norm(z[:, C_Q:C_Q + D_ATT], gsum, qg_ref[...]) * (HEAD_DIM ** -0.5 * LOG2E)
        kn = _head_norm(z[:, C_K:C_K + D_ATT], gsum, kg_ref[...])
        qt = qn.T
        vt = z[:, C_V:C_V + D_ATT].T
        kt_ref[0, :, rs] = kn.T
        vt_ref[0, :, rs] = vt

        logf = jnp.where(lane < N_HEADS, _log_sigmoid(z[:, C_F:C_F + LANES] + bf_ref[...]), 0.0)
        logft_ref[0, :, rs] = logf.T[0:N_HEADS, :]
        f_hi, f_mid, f_lo = _split3(logf)
        f3 = (f_hi + pltpu.roll(f_mid, 8, 1) + pltpu.roll(f_lo, 16, 1)).astype(BF16)
        cs3 = jnp.dot(ltri_ref[0:sub, 0:sub], f3, preferred_element_type=F32)
        c = cs3 + pltpu.roll(cs3, LANES - 8, 1) + pltpu.roll(cs3, LANES - 16, 1) + carry_ref[...]
        c = jnp.where(lane < N_HEADS, c, 0.0)
        carry_ref[...] = c[sub - 1:sub, :]
        c2 = c * LOG2E
        c_hi, c_mid, c_lo = _split3(c2)
        ct_hi, ct_mid, ct_lo = _split3(c2.T[0:N_HEADS, :])

        for hd in range(N_HEADS):
            p, odd = hd // 2, hd % 2
            base = 0 if odd else 64
            ch, cm, cl = c_hi[:, hd:hd + 1], c_mid[:, hd:hd + 1], c_lo[:, hd:hd + 1]
            ones_k = (lane >= base) & (lane < base + 3)
            ek = jnp.where(lane == base + 3, -ch, jnp.where(lane == base + 4, -cm, jnp.where(lane == base + 5, -cl,
                           jnp.where(ones_k, 1.0, 0.0))))
            val = (lane >= 64) if odd else (lane < 64)
            ka_ref[0, hd, rs, :] = jnp.where(val, kn[:, p * LANES:(p + 1) * LANES], ek).astype(BF16)
            eq = jnp.where(row8 == 0, ct_hi[hd:hd + 1, :], jnp.where(row8 == 1, ct_mid[hd:hd + 1, :],
                           jnp.where(row8 == 2, ct_lo[hd:hd + 1, :], jnp.where(row8 < 6, 1.0, 0.0))))
            qh = qt[hd * HEAD_DIM:(hd + 1) * HEAD_DIM, :]
            parts = [eq, pad_rows, qh] if odd else [qh, eq, pad_rows]
            qt_ref[0, hd, :, rs] = jnp.concatenate(parts, axis=0).astype(BF16)
            vtb_ref[0, hd, 0, :, rs] = vt[hd * HEAD_DIM:(hd + 1) * HEAD_DIM, :].astype(BF16)

        u = z[:, C_POOL:C_POOL + 256]
        uh_ref[POOL_HIST + r0:POOL_HIST + r0 + sub, :] = u
        pos = t * tm + r0 + lax.broadcasted_iota(jnp.int32, (sub, LANES), 0)
        n_ext = POOL_HIST + sub
        mixed = []
        for cb in range(2):
            wl = _pool_window_lanes((sub, LANES), cb)
            e = uh_ref[r0:r0 + n_ext, cb * LANES:(cb + 1) * LANES]
            s2 = e[8:n_ext] + e[7:n_ext - 1]
            s4 = s2[8:] + s2[6:n_ext - 10]
            if cb == 0:
                lo, hi = s2[24:24 + sub], s4[16:16 + sub]
            else:
                s8 = s4[8:] + s4[4:n_ext - 20]
                s16 = s8[8:] + s8[0:n_ext - 32]
                lo, hi = s8[8:8 + sub], s16
            acc = jnp.where(lane < 64, lo, hi)
            cnt = jnp.minimum(pos + 1, wl).astype(F32)
            mixed.append(acc / cnt - u[:, cb * LANES:(cb + 1) * LANES])
        mixed = jnp.concatenate(mixed, axis=1).astype(BF16)
        pm = jnp.dot(mixed, poolw_ref[...], preferred_element_type=F32) * pscale_ref[...]
        pm_ref[0, rs, :] = pm.astype(BF16)

        glu = z[:, C_CONV:C_CONV + 256] * jax.nn.sigmoid(z[:, C_CONV + 256:C_CONV + 512])
        gh_ref[CONV_HIST + r0:CONV_HIST + r0 + sub, :] = glu
        y = jnp.zeros((sub, 256), F32) + dwb_ref[...]
        for r in range(8):
            offs = [o for o in range(first, first + CONV_WIDTH) if o % 8 == r]
            a_max = max(offs) // 8
            if r:
                sh_ref[0:sub + 8 * a_max, :] = gh_ref[r0 + r:r0 + r + sub + 8 * a_max, :]
            for o in offs:
                a = o // 8
                tap = sh_ref[8 * a:8 * a + sub, :] if r else gh_ref[r0 + 8 * a:r0 + 8 * a + sub, :]
                y = y + dww_ref[o - first:o - first + 1, :] * tap
        ca_ref[0, rs, :] = _layer_norm_silu(y, lng_ref[...], lnb_ref[...]).astype(BF16)

    ptail_ref[0] = uh_ref[tm:tm + POOL_HIST, :]
    uh_ref[0:POOL_HIST, :] = uh_ref[tm:tm + POOL_HIST, :]
    ctail_ref[0] = gh_ref[tm:tm + CONV_HIST, :]
    gh_ref[0:CONV_HIST, :] = gh_ref[tm:tm + CONV_HIST, :]


def _inproj_prompt(x, lw, tm, layer, depth, prev):
    B, T, D = x.shape
    nt = T // tm
    const = lambda shape: pl.BlockSpec(shape, lambda b, t: (0,) * len(shape))
    row = lambda w: pl.BlockSpec((1, tm, w), lambda b, t: (b, t, 0))
    tail = lambda r: pl.BlockSpec((1, r, 256), lambda b, t: (b, 0, 0))
    stack = lambda r: pl.BlockSpec((None, 1, r, tm), lambda b, t: (layer, b, 0, t))
    out_shape = (
        jax.ShapeDtypeStruct((B, N_HEADS, LANES, T), BF16),
        jax.ShapeDtypeStruct((B, N_HEADS, T, LANES), BF16),
        jax.ShapeDtypeStruct((B, N_HEADS, nt, HEAD_DIM, tm), BF16),
        jax.ShapeDtypeStruct((depth, B, D_ATT, T), F32),
        jax.ShapeDtypeStruct((depth, B, D_ATT, T), F32),
        jax.ShapeDtypeStruct((depth, B, N_HEADS, T), F32),
        jax.ShapeDtypeStruct((B, T, 256), BF16),
        jax.ShapeDtypeStruct((B, T, 256), BF16),
        jax.ShapeDtypeStruct((B, POOL_HIST, 256), F32),
        jax.ShapeDtypeStruct((B, CONV_HIST, 256), F32),
    )
    out_specs = (pl.BlockSpec((1, N_HEADS, LANES, tm), lambda b, t: (b, 0, 0, t)),
                 pl.BlockSpec((1, N_HEADS, tm, LANES), lambda b, t: (b, 0, t, 0)),
                 pl.BlockSpec((1, N_HEADS, 1, HEAD_DIM, tm), lambda b, t: (b, 0, t, 0, 0)),
                 stack(D_ATT), stack(D_ATT), stack(N_HEADS), row(256), row(256), tail(POOL_HIST), tail(CONV_HIST))
    in_specs = [row(D), const((1, D)), const((D, D_MAIN)), const((1, LANES)), const((1, D_ATT)), const((1, D_ATT)),
                const((256, 256)), const((tm, tm)), const((256, 256)), const((1, 256)), const((CONV_HIST, 256)),
                const((1, 256)), const((1, 256)), const((1, 256))]
    n_in = len(in_specs)
    in_specs += [pl.BlockSpec(memory_space=pl.ANY)] * len(prev)
    aliases = {n_in + i: 3 + i for i in range(len(prev))}
    return pl.pallas_call(
        functools.partial(_inproj_prompt_kernel, tm=tm, n_prev=len(prev)),
        grid=(B, nt), in_specs=in_specs, out_specs=out_specs, out_shape=out_shape,
        scratch_shapes=[pltpu.VMEM((POOL_HIST + tm, 256), F32), pltpu.VMEM((CONV_HIST + tm, 256), F32),
                        pltpu.VMEM((CONV_HIST + tm, 256), F32), pltpu.VMEM((1, LANES), F32)],
        input_output_aliases=aliases,
        compiler_params=_cparams(2), name="inproj_prompt",
    )(x, lw["n1"], lw["w_main"], lw["bf"], lw["qg"], lw["kg"], lw["gsum"], lw["ltri"], lw["poolw"],
      lw["pscale"], lw["dww"], lw["dwb"], lw["lng"], lw["lnb"], *prev)


def _attn_prompt_kernel(qt_ref, ka_ref, vt_ref, o_ref, s_ref, p_ref, m_ref, l_ref, acc_ref, mx_ref, *, tq):
    i = pl.program_id(2)
    n_chunk = tq // S_CHUNK
    for hh in range(ATT_HEADS):
        m_ref[hh] = jnp.full((1, tq), -jnp.inf, F32)
        l_ref[hh] = jnp.zeros((1, tq), F32)
        acc_ref[hh] = jnp.zeros((HEAD_DIM, tq), F32)

    def score(hh, j, masked):
        start = pl.multiple_of(j * tq, tq)
        for r in range(n_chunk):
            k = ka_ref[0, hh, pl.ds(start + r * S_CHUNK, S_CHUNK), :]
            st = jnp.dot(k, qt_ref[0, hh], preferred_element_type=F32)
            if masked:
                key = lax.broadcasted_iota(jnp.int32, (S_CHUNK, tq), 0) + r * S_CHUNK
                qry = lax.broadcasted_iota(jnp.int32, (S_CHUNK, tq), 1)
                st = jnp.where(key <= qry, st, -jnp.inf)
            s_ref[hh, r * S_CHUNK:(r + 1) * S_CHUNK, :] = st
            cm = jnp.max(st, axis=0, keepdims=True)
            mx = cm if r == 0 else jnp.maximum(mx, cm)
        mx_ref[hh] = mx

    def softmax(hh):
        m_old = m_ref[hh]
        m_new = jnp.maximum(m_old, mx_ref[hh])
        p = jnp.exp2(s_ref[hh] - m_new)
        alpha = jnp.exp2(m_old - m_new)
        l_ref[hh] = alpha * l_ref[hh] + jnp.sum(p, axis=0, keepdims=True)
        m_ref[hh] = m_new
        p_ref[hh] = p.astype(BF16)
        return alpha

    def pv(hh, jv, alpha):
        acc_ref[hh] = alpha * acc_ref[hh] + jnp.dot(vt_ref[0, hh, jv], p_ref[hh],
                                                   preferred_element_type=F32)

    for hh in range(ATT_HEADS):
        score(hh, i, True)

    def body(t, carry):
        prev = jnp.where(t == 0, i, t - 1)
        for hh in range(ATT_HEADS):
            alpha = softmax(hh)
            score(hh, t, False)
            pv(hh, prev, alpha)
        return carry

    lax.fori_loop(0, i, body, 0)
    last = jnp.maximum(i - 1, 0)
    for hh in range(ATT_HEADS):
        pv(hh, last, softmax(hh))
    o = jnp.concatenate([acc_ref[hh] / l_ref[hh] for hh in range(ATT_HEADS)], axis=0)
    o_ref[0] = o.astype(BF16)


def _attn_prompt(qt, ka, vtb, tq):
    B, _, _, T = qt.shape
    nq = T // tq
    nh = ATT_HEADS
    return pl.pallas_call(
        functools.partial(_attn_prompt_kernel, tq=tq),
        grid=(B, N_HEADS // nh, nq),
        in_specs=[pl.BlockSpec((1, nh, LANES, tq), lambda b, p, i: (b, p, 0, i)),
                  pl.BlockSpec((1, nh, T, LANES), lambda b, p, i: (b, p, 0, 0)),
                  pl.BlockSpec((1, nh, nq, HEAD_DIM, tq), lambda b, p, i: (b, p, 0, 0, 0))],
        out_specs=pl.BlockSpec((1, nh * HEAD_DIM, tq), lambda b, p, i: (b, p, i)),
        out_shape=jax.ShapeDtypeStruct((B, D_ATT, T), BF16),
        scratch_shapes=[pltpu.VMEM((nh, tq, tq), F32), pltpu.VMEM((nh, tq, tq), BF16),
                        pltpu.VMEM((nh, 1, tq), F32), pltpu.VMEM((nh, 1, tq), F32),
                        pltpu.VMEM((nh, HEAD_DIM, tq), F32), pltpu.VMEM((nh, 1, tq), F32)],
        compiler_params=_cparams(3), name="attn_prompt",
    )(qt, ka, vtb)


def _merge_kernel(x_ref, att_ref, pm_ref, ca_ref, n1_ref, wg_ref, bg_ref, wau_ref, wpu_ref, wco_ref, wo_ref, o_ref):
    x = x_ref[0]
    d = x.shape[1]
    h = _rms(x, n1_ref[...]).astype(BF16)
    branches = (lax.dot_general(att_ref[0], wau_ref[...], (((0,), (0,)), ((), ())), preferred_element_type=F32),
                jnp.dot(pm_ref[0], wpu_ref[...], preferred_element_type=F32),
                jnp.dot(ca_ref[0], wco_ref[...], preferred_element_type=F32))
    merged = jnp.zeros_like(x)
    for br in range(3):
        gl = jnp.dot(h, wg_ref[:, br * d:(br + 1) * d], preferred_element_type=F32) + bg_ref[br:br + 1, :]
        merged = merged + jax.nn.sigmoid(gl) * branches[br]
    o_ref[0] = x + jnp.dot(merged.astype(BF16), wo_ref[...], preferred_element_type=F32)


def _merge(x, att_t, pm, ca, lw, tm):
    B, T, d = x.shape
    const = lambda shape: pl.BlockSpec(shape, lambda b, t: (0, 0))
    row = lambda w: pl.BlockSpec((1, tm, w), lambda b, t: (b, t, 0))
    return pl.pallas_call(
        _merge_kernel, grid=(B, T // tm),
        in_specs=[row(d), pl.BlockSpec((1, D_ATT, tm), lambda b, t: (b, 0, t)), row(256), row(256),
                  const((1, d)), const((d, 3 * d)), const((3, d)),
                  const((D_ATT, d)), const((256, d)), const((256, d)), const((d, d))],
        out_specs=row(d), out_shape=jax.ShapeDtypeStruct((B, T, d), F32),
        compiler_params=_cparams(2), name="merge",
    )(x, att_t, pm, ca, lw["n1"], lw["w_gate"], lw["bg"], lw["w_att_up"], lw["w_pool_up"], lw["w_conv_out"], lw["w_out"])


def _mlp_kernel(x_ref, n2_ref, wu_ref, wd_ref, o_ref):
    x = x_ref[...]
    h = _rms(x, n2_ref[...]).astype(BF16)
    a = jnp.maximum(jnp.dot(h, wu_ref[...], preferred_element_type=F32), 0.0)
    hid = (a * a).astype(BF16)
    o_ref[...] = x + jnp.dot(hid, wd_ref[...], preferred_element_type=F32)


def _mlp(x, lw, tm):
    n, d = x.shape
    dff = lw["w_up"].shape[1]
    single = pl.Buffered(1)
    return pl.pallas_call(
        _mlp_kernel, grid=(n // tm,),
        in_specs=[pl.BlockSpec((tm, d), lambda t: (t, 0)), pl.BlockSpec((1, d), lambda t: (0, 0)),
                  pl.BlockSpec((d, dff), lambda t: (0, 0), pipeline_mode=single),
                  pl.BlockSpec((dff, d), lambda t: (0, 0), pipeline_mode=single)],
        out_specs=pl.BlockSpec((tm, d), lambda t: (t, 0)), out_shape=jax.ShapeDtypeStruct((n, d), F32),
        compiler_params=_cparams(1), name="mlp",
    )(x, lw["n2"], lw["w_up"], lw["w_down"])


def _inproj_sample_kernel(x_ref, n1_ref, w_ref, bf_ref, qg_ref, kg_ref, gsum_ref, poolw_ref, pscale_ref,
                          dww_ref, dwb_ref, lng_ref, lnb_ref, sp_ref, sc_ref,
                          q_ref, k_ref, v_ref, logf_ref, u_ref, glu_ref, pm_ref, ca_ref, *, start_pos):
    x = x_ref[...]
    n = x.shape[0]
    h = _rms(x, n1_ref[...]).astype(BF16)
    z = jnp.dot(h, w_ref[...], preferred_element_type=F32)
    gsum = gsum_ref[...]
    q_ref[...] = (_head_norm(z[:, C_Q:C_Q + D_ATT], gsum, qg_ref[...]) * (HEAD_DIM ** -0.5)).astype(BF16)
    k_ref[...] = _head_norm(z[:, C_K:C_K + D_ATT], gsum, kg_ref[...])
    v_ref[...] = z[:, C_V:C_V + D_ATT]
    logf_ref[...] = _log_sigmoid(z[:, C_F:C_F + LANES] + bf_ref[...])

    u = z[:, C_POOL:C_POOL + 256]
    u_ref[...] = u
    mixed = []
    for cb in range(2):
        wl = _pool_window_lanes((n, LANES), cb)
        acc = u[:, cb * LANES:(cb + 1) * LANES]
        for j in range(1, POOL_WINDOWS[2 * cb + 1]):
            acc = acc + jnp.where(wl > j, sp_ref[POOL_BUF - j, :, cb * LANES:(cb + 1) * LANES], 0.0)
        cnt = jnp.minimum(start_pos + 1, wl).astype(F32)
        mixed.append(acc / cnt - u[:, cb * LANES:(cb + 1) * LANES])
    mixed = jnp.concatenate(mixed, axis=1).astype(BF16)
    pm_ref[...] = (jnp.dot(mixed, poolw_ref[...], preferred_element_type=F32) * pscale_ref[...]).astype(BF16)

    glu = z[:, C_CONV:C_CONV + 256] * jax.nn.sigmoid(z[:, C_CONV + 256:C_CONV + 512])
    glu_ref[...] = glu
    y = dwb_ref[...] + dww_ref[CONV_BUF:CONV_BUF + 1, :] * glu
    for j in range(CONV_BUF):
        y = y + dww_ref[j:j + 1, :] * sc_ref[j]
    ca_ref[...] = _layer_norm_silu(y, lng_ref[...], lnb_ref[...]).astype(BF16)


def _inproj_sample(x, lw, sp_t, sc_t, start_pos):
    n, d = x.shape
    sds = jax.ShapeDtypeStruct
    out_shape = (sds((n, D_ATT), BF16), sds((n, D_ATT), F32), sds((n, D_ATT), F32), sds((n, LANES), F32),
                 sds((n, 256), F32), sds((n, 256), F32), sds((n, 256), BF16), sds((n, 256), BF16))
    return pl.pallas_call(
        functools.partial(_inproj_sample_kernel, start_pos=start_pos),
        out_shape=out_shape,
        compiler_params=pltpu.CompilerParams(vmem_limit_bytes=VMEM_LIMIT), name="inproj_sample",
    )(x, lw["n1"], lw["w_main"], lw["bf"], lw["qg"], lw["kg"], lw["gsum"], lw["poolw"], lw["pscale"],
      lw["dww"], lw["dwb"], lw["lng"], lw["lnb"], sp_t, sc_t)


def _attn_sample_kernel(pt_ref, q_ref, kn_ref, vn_ref, fn_ref, tri_ref, *refs, pp, page):
    k_refs, v_refs, f_refs = refs[:pp], refs[pp:2 * pp], refs[2 * pp:3 * pp]
    o_ref = refs[3 * pp]
    m_ref, l_ref, acc_ref, carry_ref = refs[3 * pp + 1:]
    j = pl.program_id(1)
    nj = pl.num_programs(1)

    @pl.when(j == 0)
    def _():
        m_ref[...] = jnp.full((N_HEADS, 1), -jnp.inf, F32)
        l_ref[...] = jnp.zeros((N_HEADS, 1), F32)
        acc_ref[...] = jnp.zeros((N_HEADS, D_ATT), F32)
        carry_ref[...] = jnp.zeros((N_HEADS, 1), F32)

    head_of_lane = lax.broadcasted_iota(jnp.int32, (N_HEADS, D_ATT), 1) // HEAD_DIM
    own = head_of_lane == lax.broadcasted_iota(jnp.int32, (N_HEADS, D_ATT), 0)
    qbd32 = jnp.where(own, jnp.broadcast_to(q_ref[0].astype(F32), (N_HEADS, D_ATT)), 0.0)
    qbd = qbd32.astype(BF16)

    f_hi, f_mid, f_lo = _split3(jnp.concatenate([f_refs[i][...] for i in range(pp)], axis=0))
    tri = tri_ref[...]
    cs_all = (jnp.dot(f_hi.astype(BF16), tri, preferred_element_type=F32)
              + jnp.dot(f_mid.astype(BF16), tri, preferred_element_type=F32)
              + jnp.dot(f_lo.astype(BF16), tri, preferred_element_type=F32))

    s_list = []
    carry = carry_ref[...]
    for i in range(pp):
        kt = k_refs[i][...].astype(BF16)
        s = jnp.dot(qbd, kt, preferred_element_type=F32)
        cs = cs_all[i * N_HEADS:(i + 1) * N_HEADS, :]
        s_list.append(s - (cs + carry))
        carry = carry + cs[:, page - 1:page]
    carry_ref[...] = carry

    m_old = m_ref[...]
    m_blk = s_list[0]
    for s in s_list[1:]:
        m_blk = jnp.maximum(m_blk, s)
    m_new = jnp.maximum(m_old, jnp.max(m_blk, axis=1, keepdims=True))
    alpha = jnp.exp(m_old - m_new)
    p_sum = jnp.zeros((N_HEADS, page), F32)
    acc = alpha * acc_ref[...]
    for i in range(pp):
        p = jnp.exp(s_list[i] - m_new)
        p_sum = p_sum + p
        acc = acc + lax.dot_general(p.astype(BF16), v_refs[i][...].astype(BF16), (((1,), (1,)), ((), ())),
                                    preferred_element_type=F32)
    l_new = alpha * l_ref[...] + jnp.sum(p_sum, axis=1, keepdims=True)
    m_ref[...] = m_new
    l_ref[...] = l_new
    acc_ref[...] = acc

    @pl.when(j == nj - 1)
    def _():
        kn = jnp.broadcast_to(kn_ref[0], (N_HEADS, D_ATT)).astype(BF16).astype(F32)
        vn = jnp.broadcast_to(vn_ref[0], (N_HEADS, D_ATT)).astype(BF16).astype(F32)
        fn = fn_ref[0]
        s_new = jnp.sum(qbd32 * kn, axis=1, keepdims=True) - (carry + fn)
        m_fin = jnp.maximum(m_new, s_new)
        a_fin = jnp.exp(m_new - m_fin)
        p_new = jnp.exp(s_new - m_fin)
        l_fin = a_fin * l_new + p_new
        acc_fin = a_fin * acc + p_new.astype(BF16).astype(F32) * vn
        o = jnp.where(own, acc_fin / l_fin, 0.0)
        o_ref[0] = jnp.sum(o, axis=0, keepdims=True).astype(BF16)


def _attn_sample(page_table, q, k_new, v_new, logf_new, tri, cache_k, cache_v, cache_f, layer, pp):
    bd, n_pages = page_table.shape
    page = cache_k.shape[3]
    nj = n_pages // pp
    tok = lambda w: pl.BlockSpec((1, 1, w), lambda b, j, pt: (b, 0, 0))
    kv_spec = lambda i: pl.BlockSpec((None, None, D_ATT, page),
                                     lambda b, j, pt, i=i: (layer, pt[b, j * pp + i], 0, 0))
    f_spec = lambda i: pl.BlockSpec((None, None, N_HEADS, page),
                                    lambda b, j, pt, i=i: (layer, pt[b, j * pp + i], 0, 0))
    in_specs = ([tok(D_ATT), tok(D_ATT), tok(D_ATT), pl.BlockSpec((1, N_HEADS, 1), lambda b, j, pt: (b, 0, 0)),
                 pl.BlockSpec((page, page), lambda b, j, pt: (0, 0))]
                + [kv_spec(i) for i in range(pp)] + [kv_spec(i) for i in range(pp)] + [f_spec(i) for i in range(pp)])
    grid_spec = pltpu.PrefetchScalarGridSpec(
        num_scalar_prefetch=1, grid=(bd, nj), in_specs=in_specs,
        out_specs=pl.BlockSpec((1, 1, D_ATT), lambda b, j, pt: (b, 0, 0)),
        scratch_shapes=[pltpu.VMEM((N_HEADS, 1), F32), pltpu.VMEM((N_HEADS, 1), F32),
                        pltpu.VMEM((N_HEADS, D_ATT), F32), pltpu.VMEM((N_HEADS, 1), F32)])
    out = pl.pallas_call(
        functools.partial(_attn_sample_kernel, pp=pp, page=page),
        grid_spec=grid_spec, out_shape=jax.ShapeDtypeStruct((bd, 1, D_ATT), BF16),
        compiler_params=_cparams(2), name="attn_sample",
    )(page_table, q[:, None, :], k_new[:, None, :], v_new[:, None, :], logf_new[:, :N_HEADS, None], tri,
      *([cache_k] * pp), *([cache_v] * pp), *([cache_f] * pp))
    return out[:, 0, :]


def _layer_weights(l, tm, norm1_g, w_in, b_f, b_gate, q_gain, k_gain, pool_w, pool_scale, w_pool_up, w_att_up,
                   dw_w, dw_b, conv_ln_g, conv_ln_b, w_conv_out, w_out, norm2_g, w_mlp_up, w_mlp_down):
    d = w_in.shape[1]
    w = w_in[l]
    o_f = 3 * D_ATT
    o_pool = o_f + N_HEADS
    o_conv = o_pool + 256
    o_gate = o_conv + 512
    w_main = jnp.concatenate([w[:, :o_f], w[:, o_pool:o_gate], w[:, o_f:o_pool],
                              jnp.zeros((d, LANES - N_HEADS), F32)], axis=1).astype(BF16)
    grp = jnp.arange(256) // HEAD_DIM
    poolw = jnp.zeros((256, 256), F32)
    for g in range(4):
        poolw = poolw.at[g * 64:(g + 1) * 64, g * 64:(g + 1) * 64].set(pool_w[l, g])
    return dict(
        n1=norm1_g[l][None, :], w_main=w_main, w_gate=w[:, o_gate:].astype(BF16),
        bf=jnp.pad(b_f[l], (0, LANES - N_HEADS))[None, :], bg=b_gate[l],
        qg=jnp.tile(q_gain[l], N_HEADS)[None, :], kg=jnp.tile(k_gain[l], N_HEADS)[None, :],
        gsum=(grp[:, None] == grp[None, :]).astype(BF16),
        ltri=(jnp.arange(tm)[:, None] >= jnp.arange(tm)[None, :]).astype(BF16),
        poolw=poolw.astype(BF16), pscale=pool_scale[l][None, :],
        dww=jnp.pad(dw_w[l], ((0, CONV_HIST - CONV_WIDTH), (0, 0))), dwb=dw_b[l][None, :],
        lng=conv_ln_g[l][None, :], lnb=conv_ln_b[l][None, :],
        w_att_up=w_att_up[l].astype(BF16), w_pool_up=w_pool_up[l].astype(BF16),
        w_conv_out=w_conv_out[l].astype(BF16), w_out=w_out[l].astype(BF16),
        n2=norm2_g[l][None, :], w_up=w_mlp_up[l].astype(BF16), w_down=w_mlp_down[l].astype(BF16))


def kernel(x_prompt, x_sample, cache_k, cache_v, cache_logf, state_pool, state_conv, page_table,
           norm1_g, w_in, b_f, b_gate, q_gain, k_gain, pool_w, pool_scale, w_pool_up, w_att_up,
           dw_w, dw_b, conv_ln_g, conv_ln_b, w_conv_out, w_out, norm2_g, w_mlp_up, w_mlp_down):
    B, T, D = x_prompt.shape
    bd = x_sample.shape[0]
    depth = w_in.shape[0]
    n_pool_pages, page = cache_k.shape[1], cache_k.shape[2]
    n_pages = page_table.shape[1]
    past_len = n_pages * page
    tm = min(512, T)
    pp = min(32, n_pages)

    tri = (jnp.arange(page)[:, None] <= jnp.arange(page)[None, :]).astype(BF16)
    ck_flat = jnp.transpose(cache_k, (0, 1, 3, 4, 2)).reshape(depth, n_pool_pages, D_ATT, page)
    cv_flat = jnp.transpose(cache_v, (0, 1, 3, 4, 2)).reshape(depth, n_pool_pages, D_ATT, page)
    cf_flat = jnp.transpose(cache_logf, (0, 1, 3, 2))

    xp = x_prompt
    xs = x_sample.reshape(bd, D)
    outs = [[] for _ in range(7)]
    stacks = ()
    for l in range(depth):
        lw = _layer_weights(l, tm, norm1_g, w_in, b_f, b_gate, q_gain, k_gain, pool_w, pool_scale, w_pool_up,
                            w_att_up, dw_w, dw_b, conv_ln_g, conv_ln_b, w_conv_out, w_out, norm2_g,
                            w_mlp_up, w_mlp_down)
        qt, ka, vtb, kt_all, vt_all, ft_all, pm, ca, ptail, ctail = _inproj_prompt(xp, lw, tm, l, depth, stacks)
        stacks = (kt_all, vt_all, ft_all)
        att_t = _attn_prompt(qt, ka, vtb, tm)
        x1 = _merge(xp, att_t, pm, ca, lw, tm)
        xp = _mlp(x1.reshape(B * T, D), lw, tm).reshape(B, T, D)
        sp_t = jnp.transpose(state_pool[l], (1, 0, 2))
        sc_t = jnp.transpose(state_conv[l], (1, 0, 2))
        qs, ks, vs, fs, us, gs, pms, cas = _inproj_sample(xs, lw, sp_t, sc_t, past_len)
        att_s = _attn_sample(page_table, qs, ks, vs, fs, tri, ck_flat, cv_flat, cf_flat, l, pp)
        xs1 = _merge(xs[None], att_s.T[None], pms[None], cas[None], lw, bd)[0]
        xs = _mlp(xs1, lw, bd)

        outs[0].append(ptail[:, POOL_HIST - POOL_BUF:])
        outs[1].append(ctail[:, CONV_HIST - CONV_BUF:])
        outs[2].append(ks.reshape(bd, 1, N_HEADS, HEAD_DIM))
        outs[3].append(vs.reshape(bd, 1, N_HEADS, HEAD_DIM))
        outs[4].append(fs[:, None, :N_HEADS])
        outs[5].append(jnp.concatenate([state_pool[l][:, 1:], us[:, None, :]], axis=1))
        outs[6].append(jnp.concatenate([state_conv[l][:, 1:], gs[:, None, :]], axis=1))
    kt_all, vt_all, ft_all = stacks
    k_prompt = jnp.transpose(kt_all.reshape(depth, B, N_HEADS, HEAD_DIM, T), (0, 1, 4, 2, 3))
    v_prompt = jnp.transpose(vt_all.reshape(depth, B, N_HEADS, HEAD_DIM, T), (0, 1, 4, 2, 3))
    f_prompt = jnp.transpose(ft_all, (0, 1, 3, 2))
    st = [jnp.stack(o) for o in outs]
    return (xp, xs.reshape(bd, 1, D), k_prompt, v_prompt, f_prompt) + tuple(st)
```

```python
import functools

import jax
import jax.numpy as jnp
from jax import lax
from jax.experimental import pallas as pl
from jax.experimental.pallas import tpu as pltpu

F32 = jnp.float32
BF16 = jnp.bfloat16

N_HEADS = 8
HEAD_DIM = 64
D_ATT = N_HEADS * HEAD_DIM
ATT_HEADS = 8
POOL_WINDOWS = (2, 4, 8, 16)
POOL_BUF = max(POOL_WINDOWS) - 1
POOL_HIST = 32
CONV_WIDTH = 31
CONV_BUF = CONV_WIDTH - 1
CONV_HIST = 32
NORM_EPS = 1e-6
LN_EPS = 1e-5
LOG2E = 1.4426950408889634
N_SUB = 2
S_CHUNK = 512
LANES = 128
VMEM_LIMIT = 56 * 1024 * 1024

C_Q, C_K, C_V, C_POOL, C_CONV, C_F = 0, 512, 1024, 1536, 1792, 2304
D_MAIN = C_F + LANES


def _cparams(n_axes):
    return pltpu.CompilerParams(dimension_semantics=("arbitrary",) * n_axes, vmem_limit_bytes=VMEM_LIMIT)


def _split3(x):
    hi = x.astype(BF16).astype(F32)
    r = x - hi
    mid = r.astype(BF16).astype(F32)
    lo = (r - mid).astype(BF16).astype(F32)
    return hi, mid, lo


def _rms(x, g):
    return x * lax.rsqrt(jnp.mean(x * x, axis=-1, keepdims=True) + NORM_EPS) * g


def _log_sigmoid(x):
    return jnp.minimum(x, 0.0) - jnp.log1p(jnp.exp(-jnp.abs(x)))


def _head_norm(a, gsum, gain):
    outs = []
    for c in range(2):
        blk = a[:, c * 256:(c + 1) * 256]
        sq = blk * blk
        hi = sq.astype(BF16)
        lo = (sq - hi.astype(F32)).astype(BF16)
        ss = jnp.dot(hi, gsum, preferred_element_type=F32) + jnp.dot(lo, gsum, preferred_element_type=F32)
        outs.append(blk * lax.rsqrt(ss * (1.0 / HEAD_DIM) + NORM_EPS))
    return jnp.concatenate(outs, axis=1) * gain


def _pool_window_lanes(shape, col_block):
    lane = lax.broadcasted_iota(jnp.int32, shape, len(shape) - 1)
    lo_w, hi_w = POOL_WINDOWS[2 * col_block], POOL_WINDOWS[2 * col_block + 1]
    return jnp.where(lane < 64, lo_w, hi_w)


def _layer_norm_silu(y, g, b):
    mu = jnp.mean(y, axis=-1, keepdims=True)
    d = y - mu
    var = jnp.mean(d * d, axis=-1, keepdims=True)
    yn = d * lax.rsqrt(var + LN_EPS) * g + b
    return yn * jax.nn.sigmoid(yn)


def _inproj_prompt_kernel(x_ref, n1_ref, w_ref, bf_ref, qg_ref, kg_ref, gsum_ref, ltri_ref, poolw_ref,
                          pscale_ref, dww_ref, dwb_ref, lng_ref, lnb_ref,
                          *rest, tm, n_prev):
    (qt_ref, ka_ref, vtb_ref, kt_ref, vt_ref, logft_ref, pm_ref, ca_ref, ptail_ref, ctail_ref,
     uh_ref, gh_ref, sh_ref, carry_ref) = rest[n_prev:]
    t = pl.program_id(1)

    @pl.when(t == 0)
    def _():
        uh_ref[0:POOL_HIST, :] = jnp.zeros((POOL_HIST, 256), F32)
        gh_ref[0:CONV_HIST, :] = jnp.zeros((CONV_HIST, 256), F32)
        carry_ref[...] = jnp.zeros((1, LANES), F32)

    sub = tm // N_SUB
    rows = [slice(i * sub, (i + 1) * sub) for i in range(N_SUB)]
    zs = [jnp.dot(_rms(x_ref[0, rs, :], n1_ref[...]).astype(BF16), w_ref[...], preferred_element_type=F32)
          for rs in rows]
    gsum = gsum_ref[...]
    lane = lax.broadcasted_iota(jnp.int32, (sub, LANES), 1)
    row8 = lax.broadcasted_iota(jnp.int32, (8, sub), 0)
    pad_rows = jnp.zeros((HEAD_DIM - 8, sub), F32)
    first = CONV_HIST - CONV_BUF

    for rs, z in zip(rows, zs):
        r0 = rs.start
        qn = _head_norm(z[:, C_Q:C_Q + D_ATT], gsum, qg_ref[...]) * (HEAD_DIM ** -0.5 * LOG2E)
        kn = _head_norm(z[:, C_K:C_K + D_ATT], gsum, kg_ref[...])
        qt = qn.T
        vt = z[:, C_V:C_V + D_ATT].T
        kt_ref[0, :, rs] = kn.T
        vt_ref[0, :, rs] = vt

        logf = jnp.where(lane < N_HEADS, _log_sigmoid(z[:, C_F:C_F + LANES] + bf_ref[...]), 0.0)
        logft_ref[0, :, rs] = logf.T[0:N_HEADS, :]
        f_hi, f_mid, f_lo = _split3(logf)
        f3 = (f_hi + pltpu.roll(f_mid, 8, 1) + pltpu.roll(f_lo, 16, 1)).astype(BF16)
        cs3 = jnp.dot(ltri_ref[0:sub, 0:sub], f3, preferred_element_type=F32)
        c = cs3 + pltpu.roll(cs3, LANES - 8, 1) + pltpu.roll(cs3, LANES - 16, 1) + carry_ref[...]
        c = jnp.where(lane < N_HEADS, c, 0.0)
        carry_ref[...] = c[sub - 1:sub, :]
        c2 = c * LOG2E
        c_hi, c_mid, c_lo = _split3(c2)
        ct_hi, ct_mid, ct_lo = _split3(c2.T[0:N_HEADS, :])

        for hd in range(N_HEADS):
            p, odd = hd // 2, hd % 2
            base = 0 if odd else 64
            ch, cm, cl = c_hi[:, hd:hd + 1], c_mid[:, hd:hd + 1], c_lo[:, hd:hd + 1]
            ones_k = (lane >= base) & (lane < base + 3)
            ek = jnp.where(lane == base + 3, -ch, jnp.where(lane == base + 4, -cm, jnp.where(lane == base + 5, -cl,
                           jnp.where(ones_k, 1.0, 0.0))))
            val = (lane >= 64) if odd else (lane < 64)
            ka_ref[0, hd, rs, :] = jnp.where(val, kn[:, p * LANES:(p + 1) * LANES], ek).astype(BF16)
            eq = jnp.where(row8 == 0, ct_hi[hd:hd + 1, :], jnp.where(row8 == 1, ct_mid[hd:hd + 1, :],
                           jnp.where(row8 == 2, ct_lo[hd:hd + 1, :], jnp.where(row8 < 6, 1.0, 0.0))))
            qh = qt[hd * HEAD_DIM:(hd + 1) * HEAD_DIM, :]
            parts = [eq, pad_rows, qh] if odd else [qh, eq, pad_rows]
            qt_ref[0, hd, :, rs] = jnp.concatenate(parts, axis=0).astype(BF16)
            vtb_ref[0, hd, 0, :, rs] = vt[hd * HEAD_DIM:(hd + 1) * HEAD_DIM, :].astype(BF16)

        u = z[:, C_POOL:C_POOL + 256]
        uh_ref[POOL_HIST + r0:POOL_HIST + r0 + sub, :] = u
        pos = t * tm + r0 + lax.broadcasted_iota(jnp.int32, (sub, LANES), 0)
        n_ext = POOL_HIST + sub
        mixed = []
        for cb in range(2):
            wl = _pool_window_lanes((sub, LANES), cb)
            e = uh_ref[r0:r0 + n_ext, cb * LANES:(cb + 1) * LANES]
            s2 = e[8:n_ext] + e[7:n_ext - 1]
            s4 = s2[8:] + s2[6:n_ext - 10]
            if cb == 0:
                lo, hi = s2[24:24 + sub], s4[16:16 + sub]
            else:
                s8 = s4[8:] + s4[4:n_ext - 20]
                s16 = s8[8:] + s8[0:n_ext - 32]
                lo, hi = s8[8:8 + sub], s16
            acc = jnp.where(lane < 64, lo, hi)
            cnt = jnp.minimum(pos + 1, wl).astype(F32)
            mixed.append(acc / cnt - u[:, cb * LANES:(cb + 1) * LANES])
        mixed = jnp.concatenate(mixed, axis=1).astype(BF16)
        pm = jnp.dot(mixed, poolw_ref[...], preferred_element_type=F32) * pscale_ref[...]
        pm_ref[0, rs, :] = pm.astype(BF16)

        glu = z[:, C_CONV:C_CONV + 256] * jax.nn.sigmoid(z[:, C_CONV + 256:C_CONV + 512])
        gh_ref[CONV_HIST + r0:CONV_HIST + r0 + sub, :] = glu
        y = jnp.zeros((sub, 256), F32) + dwb_ref[...]
        for r in range(8):
            offs = [o for o in range(first, first + CONV_WIDTH) if o % 8 == r]
            a_max = max(offs) // 8
            if r:
                sh_ref[0:sub + 8 * a_max, :] = gh_ref[r0 + r:r0 + r + sub + 8 * a_max, :]
            for o in offs:
                a = o // 8
                tap = sh_ref[8 * a:8 * a + sub, :] if r else gh_ref[r0 + 8 * a:r0 + 8 * a + sub, :]
                y = y + dww_ref[o - first:o - first + 1, :] * tap
        ca_ref[0, rs, :] = _layer_norm_silu(y, lng_ref[...], lnb_ref[...]).astype(BF16)

    ptail_ref[0] = uh_ref[tm:tm + POOL_HIST, :]
    uh_ref[0:POOL_HIST, :] = uh_ref[tm:tm + POOL_HIST, :]
    ctail_ref[0] = gh_ref[tm:tm + CONV_HIST, :]
    gh_ref[0:CONV_HIST, :] = gh_ref[tm:tm + CONV_HIST, :]


def _inproj_prompt(x, lw, tm, layer, depth, prev):
    B, T, D = x.shape
    nt = T // tm
    const = lambda shape: pl.BlockSpec(shape, lambda b, t: (0,) * len(shape))
    row = lambda w: pl.BlockSpec((1, tm, w), lambda b, t: (b, t, 0))
    tail = lambda r: pl.BlockSpec((1, r, 256), lambda b, t: (b, 0, 0))
    stack = lambda r: pl.BlockSpec((None, 1, r, tm), lambda b, t: (layer, b, 0, t))
    out_shape = (
        jax.ShapeDtypeStruct((B, N_HEADS, LANES, T), BF16),
        jax.ShapeDtypeStruct((B, N_HEADS, T, LANES), BF16),
        jax.ShapeDtypeStruct((B, N_HEADS, nt, HEAD_DIM, tm), BF16),
        jax.ShapeDtypeStruct((depth, B, D_ATT, T), F32),
        jax.ShapeDtypeStruct((depth, B, D_ATT, T), F32),
        jax.ShapeDtypeStruct((depth, B, N_HEADS, T), F32),
        jax.ShapeDtypeStruct((B, T, 256), BF16),
        jax.ShapeDtypeStruct((B, T, 256), BF16),
        jax.ShapeDtypeStruct((B, POOL_HIST, 256), F32),
        jax.ShapeDtypeStruct((B, CONV_HIST, 256), F32),
    )
    out_specs = (pl.BlockSpec((1, N_HEADS, LANES, tm), lambda b, t: (b, 0, 0, t)),
                 pl.BlockSpec((1, N_HEADS, tm, LANES), lambda b, t: (b, 0, t, 0)),
                 pl.BlockSpec((1, N_HEADS, 1, HEAD_DIM, tm), lambda b, t: (b, 0, t, 0, 0)),
                 stack(D_ATT), stack(D_ATT), stack(N_HEADS), row(256), row(256), tail(POOL_HIST), tail(CONV_HIST))
    in_specs = [row(D), const((1, D)), const((D, D_MAIN)), const((1, LANES)), const((1, D_ATT)), const((1, D_ATT)),
                const((256, 256)), const((tm, tm)), const((256, 256)), const((1, 256)), const((CONV_HIST, 256)),
                const((1, 256)), const((1, 256)), const((1, 256))]
    n_in = len(in_specs)
    in_specs += [pl.BlockSpec(memory_space=pl.ANY)] * len(prev)
    aliases = {n_in + i: 3 + i for i in range(len(prev))}
    return pl.pallas_call(
        functools.partial(_inproj_prompt_kernel, tm=tm, n_prev=len(prev)),
        grid=(B, nt), in_specs=in_specs, out_specs=out_specs, out_shape=out_shape,
        scratch_shapes=[pltpu.VMEM((POOL_HIST + tm, 256), F32), pltpu.VMEM((CONV_HIST + tm, 256), F32),
                        pltpu.VMEM((CONV_HIST + tm, 256), F32), pltpu.VMEM((1, LANES), F32)],
        input_output_aliases=aliases,
        compiler_params=_cparams(2), name="inproj_prompt",
    )(x, lw["n1"], lw["w_main"], lw["bf"], lw["qg"], lw["kg"], lw["gsum"], lw["ltri"], lw["poolw"],
      lw["pscale"], lw["dww"], lw["dwb"], lw["lng"], lw["lnb"], *prev)


def _attn_prompt_kernel(qt_ref, ka_ref, vt_ref, o_ref, s_ref, p_ref, m_ref, l_ref, acc_ref, mx_ref, *, tq):
    i = pl.program_id(2)
    n_chunk = tq // S_CHUNK
    for hh in range(ATT_HEADS):
        m_ref[hh] = jnp.full((1, tq), -jnp.inf, F32)
        l_ref[hh] = jnp.zeros((1, tq), F32)
        acc_ref[hh] = jnp.zeros((HEAD_DIM, tq), F32)

    def score(hh, j, masked):
        start = pl.multiple_of(j * tq, tq)
        for r in range(n_chunk):
            k = ka_ref[0, hh, pl.ds(start + r * S_CHUNK, S_CHUNK), :]
            st = jnp.dot(k, qt_ref[0, hh], preferred_element_type=F32)
            if masked:
                key = lax.broadcasted_iota(jnp.int32, (S_CHUNK, tq), 0) + r * S_CHUNK
                qry = lax.broadcasted_iota(jnp.int32, (S_CHUNK, tq), 1)
                st = jnp.where(key <= qry, st, -jnp.inf)
            s_ref[hh, r * S_CHUNK:(r + 1) * S_CHUNK, :] = st
            cm = jnp.max(st, axis=0, keepdims=True)
            mx = cm if r == 0 else jnp.maximum(mx, cm)
        mx_ref[hh] = mx

    def softmax(hh):
        m_old = m_ref[hh]
        m_new = jnp.maximum(m_old, mx_ref[hh])
        p = jnp.exp2(s_ref[hh] - m_new)
        alpha = jnp.exp2(m_old - m_new)
        l_ref[hh] = alpha * l_ref[hh] + jnp.sum(p, axis=0, keepdims=True)
        m_ref[hh] = m_new
        p_ref[hh] = p.astype(BF16)
        return alpha

    def pv(hh, jv, alpha):
        acc_ref[hh] = alpha * acc_ref[hh] + jnp.dot(vt_ref[0, hh, jv], p_ref[hh],
                                                   preferred_element_type=F32)

    for hh in range(ATT_HEADS):
        score(hh, i, True)

    def body(t, carry):
        prev = jnp.where(t == 0, i, t - 1)
        for hh in range(ATT_HEADS):
            alpha = softmax(hh)
            score(hh, t, False)
            pv(hh, prev, alpha)
        return carry

    lax.fori_loop(0, i, body, 0)
    last = jnp.maximum(i - 1, 0)
    for hh in range(ATT_HEADS):
        pv(hh, last, softmax(hh))
    o = jnp.concatenate([acc_ref[hh] / l_ref[hh] for hh in range(ATT_HEADS)], axis=0)
    o_ref[0] = o.astype(BF16)


def _attn_prompt(qt, ka, vtb, tq):
    B, _, _, T = qt.shape
    nq = T // tq
    nh = ATT_HEADS
    return pl.pallas_call(
        functools.partial(_attn_prompt_kernel, tq=tq),
        grid=(B, N_HEADS // nh, nq),
        in_specs=[pl.BlockSpec((1, nh, LANES, tq), lambda b, p, i: (b, p, 0, i)),
                  pl.BlockSpec((1, nh, T, LANES), lambda b, p, i: (b, p, 0, 0), pipeline_mode=pl.Buffered(1)),
                  pl.BlockSpec((1, nh, nq, HEAD_DIM, tq), lambda b, p, i: (b, p, 0, 0, 0),
                               pipeline_mode=pl.Buffered(1))],
        out_specs=pl.BlockSpec((1, nh * HEAD_DIM, tq), lambda b, p, i: (b, p, i)),
        out_shape=jax.ShapeDtypeStruct((B, D_ATT, T), BF16),
        scratch_shapes=[pltpu.VMEM((nh, tq, tq), F32), pltpu.VMEM((nh, tq, tq), BF16),
                        pltpu.VMEM((nh, 1, tq), F32), pltpu.VMEM((nh, 1, tq), F32),
                        pltpu.VMEM((nh, HEAD_DIM, tq), F32), pltpu.VMEM((nh, 1, tq), F32)],
        compiler_params=_cparams(3), name="attn_prompt",
    )(qt, ka, vtb)


def _merge_kernel(x_ref, att_ref, pm_ref, ca_ref, n1_ref, wg_ref, bg_ref, wau_ref, wpu_ref, wco_ref, wo_ref, o_ref):
    x = x_ref[0]
    d = x.shape[1]
    h = _rms(x, n1_ref[...]).astype(BF16)
    branches = (lax.dot_general(att_ref[0], wau_ref[...], (((0,), (0,)), ((), ())), preferred_element_type=F32),
                jnp.dot(pm_ref[0], wpu_ref[...], preferred_element_type=F32),
                jnp.dot(ca_ref[0], wco_ref[...], preferred_element_type=F32))
    merged = jnp.zeros_like(x)
    for br in range(3):
        gl = jnp.dot(h, wg_ref[:, br * d:(br + 1) * d], preferred_element_type=F32) + bg_ref[br:br + 1, :]
        merged = merged + jax.nn.sigmoid(gl) * branches[br]
    o_ref[0] = x + jnp.dot(merged.astype(BF16), wo_ref[...], preferred_element_type=F32)


def _merge(x, att_t, pm, ca, lw, tm):
    B, T, d = x.shape
    const = lambda shape: pl.BlockSpec(shape, lambda b, t: (0, 0))
    row = lambda w: pl.BlockSpec((1, tm, w), lambda b, t: (b, t, 0))
    return pl.pallas_call(
        _merge_kernel, grid=(B, T // tm),
        in_specs=[row(d), pl.BlockSpec((1, D_ATT, tm), lambda b, t: (b, 0, t)), row(256), row(256),
                  const((1, d)), const((d, 3 * d)), const((3, d)),
                  const((D_ATT, d)), const((256, d)), const((256, d)), const((d, d))],
        out_specs=row(d), out_shape=jax.ShapeDtypeStruct((B, T, d), F32),
        compiler_params=_cparams(2), name="merge",
    )(x, att_t, pm, ca, lw["n1"], lw["w_gate"], lw["bg"], lw["w_att_up"], lw["w_pool_up"], lw["w_conv_out"], lw["w_out"])


def _mlp_kernel(x_ref, n2_ref, wu_ref, wd_ref, o_ref):
    x = x_ref[...]
    h = _rms(x, n2_ref[...]).astype(BF16)
    a = jnp.maximum(jnp.dot(h, wu_ref[...], preferred_element_type=F32), 0.0)
    hid = (a * a).astype(BF16)
    o_ref[...] = x + jnp.dot(hid, wd_ref[...], preferred_element_type=F32)


def _mlp(x, lw, tm):
    n, d = x.shape
    dff = lw["w_up"].shape[1]
    single = pl.Buffered(1)
    return pl.pallas_call(
        _mlp_kernel, grid=(n // tm,),
        in_specs=[pl.BlockSpec((tm, d), lambda t: (t, 0)), pl.BlockSpec((1, d), lambda t: (0, 0)),
                  pl.BlockSpec((d, dff), lambda t: (0, 0), pipeline_mode=single),
                  pl.BlockSpec((dff, d), lambda t: (0, 0), pipeline_mode=single)],
        out_specs=pl.BlockSpec((tm, d), lambda t: (t, 0)), out_shape=jax.ShapeDtypeStruct((n, d), F32),
        compiler_params=_cparams(1), name="mlp",
    )(x, lw["n2"], lw["w_up"], lw["w_down"])


def _inproj_sample_kernel(x_ref, n1_ref, w_ref, bf_ref, qg_ref, kg_ref, gsum_ref, poolw_ref, pscale_ref,
                          dww_ref, dwb_ref, lng_ref, lnb_ref, sp_ref, sc_ref,
                          q_ref, k_ref, v_ref, logf_ref, u_ref, glu_ref, pm_ref, ca_ref, *, start_pos):
    x = x_ref[...]
    n = x.shape[0]
    h = _rms(x, n1_ref[...]).astype(BF16)
    z = jnp.dot(h, w_ref[...], preferred_element_type=F32)
    gsum = gsum_ref[...]
    q_ref[...] = (_head_norm(z[:, C_Q:C_Q + D_ATT], gsum, qg_ref[...]) * (HEAD_DIM ** -0.5)).astype(BF16)
    k_ref[...] = _head_norm(z[:, C_K:C_K + D_ATT], gsum, kg_ref[...])
    v_ref[...] = z[:, C_V:C_V + D_ATT]
    logf_ref[...] = _log_sigmoid(z[:, C_F:C_F + LANES] + bf_ref[...])

    u = z[:, C_POOL:C_POOL + 256]
    u_ref[...] = u
    mixed = []
    for cb in range(2):
        wl = _pool_window_lanes((n, LANES), cb)
        acc = u[:, cb * LANES:(cb + 1) * LANES]
        for j in range(1, POOL_WINDOWS[2 * cb + 1]):
            acc = acc + jnp.where(wl > j, sp_ref[POOL_BUF - j, :, cb * LANES:(cb + 1) * LANES], 0.0)
        cnt = jnp.minimum(start_pos + 1, wl).astype(F32)
        mixed.append(acc / cnt - u[:, cb * LANES:(cb + 1) * LANES])
    mixed = jnp.concatenate(mixed, axis=1).astype(BF16)
    pm_ref[...] = (jnp.dot(mixed, poolw_ref[...], preferred_element_type=F32) * pscale_ref[...]).astype(BF16)

    glu = z[:, C_CONV:C_CONV + 256] * jax.nn.sigmoid(z[:, C_CONV + 256:C_CONV + 512])
    glu_ref[...] = glu
    y = dwb_ref[...] + dww_ref[CONV_BUF:CONV_BUF + 1, :] * glu
    for j in range(CONV_BUF):
        y = y + dww_ref[j:j + 1, :] * sc_ref[j]
    ca_ref[...] = _layer_norm_silu(y, lng_ref[...], lnb_ref[...]).astype(BF16)


def _inproj_sample(x, lw, sp_t, sc_t, start_pos):
    n, d = x.shape
    sds = jax.ShapeDtypeStruct
    out_shape = (sds((n, D_ATT), BF16), sds((n, D_ATT), F32), sds((n, D_ATT), F32), sds((n, LANES), F32),
                 sds((n, 256), F32), sds((n, 256), F32), sds((n, 256), BF16), sds((n, 256), BF16))
    return pl.pallas_call(
        functools.partial(_inproj_sample_kernel, start_pos=start_pos),
        out_shape=out_shape,
        compiler_params=pltpu.CompilerParams(vmem_limit_bytes=VMEM_LIMIT), name="inproj_sample",
    )(x, lw["n1"], lw["w_main"], lw["bf"], lw["qg"], lw["kg"], lw["gsum"], lw["poolw"], lw["pscale"],
      lw["dww"], lw["dwb"], lw["lng"], lw["lnb"], sp_t, sc_t)


def _attn_sample_kernel(pt_ref, q_ref, kn_ref, vn_ref, fn_ref, tri_ref, *refs, pp, page):
    k_refs, v_refs, f_refs = refs[:pp], refs[pp:2 * pp], refs[2 * pp:3 * pp]
    o_ref = refs[3 * pp]
    m_ref, l_ref, acc_ref, carry_ref = refs[3 * pp + 1:]
    j = pl.program_id(1)
    nj = pl.num_programs(1)

    @pl.when(j == 0)
    def _():
        m_ref[...] = jnp.full((N_HEADS, 1), -jnp.inf, F32)
        l_ref[...] = jnp.zeros((N_HEADS, 1), F32)
        acc_ref[...] = jnp.zeros((N_HEADS, D_ATT), F32)
        carry_ref[...] = jnp.zeros((N_HEADS, 1), F32)

    head_of_lane = lax.broadcasted_iota(jnp.int32, (N_HEADS, D_ATT), 1) // HEAD_DIM
    own = head_of_lane == lax.broadcasted_iota(jnp.int32, (N_HEADS, D_ATT), 0)
    qbd32 = jnp.where(own, jnp.broadcast_to(q_ref[0].astype(F32), (N_HEADS, D_ATT)), 0.0)
    qbd = qbd32.astype(BF16)

    f_hi, f_mid, f_lo = _split3(jnp.concatenate([f_refs[i][...] for i in range(pp)], axis=0))
    tri = tri_ref[...]
    cs_all = (jnp.dot(f_hi.astype(BF16), tri, preferred_element_type=F32)
              + jnp.dot(f_mid.astype(BF16), tri, preferred_element_type=F32)
              + jnp.dot(f_lo.astype(BF16), tri, preferred_element_type=F32))

    s_list = []
    carry = carry_ref[...]
    for i in range(pp):
        kt = k_refs[i][...].astype(BF16)
        s = jnp.dot(qbd, kt, preferred_element_type=F32)
        cs = cs_all[i * N_HEADS:(i + 1) * N_HEADS, :]
        s_list.append(s - (cs + carry))
        carry = carry + cs[:, page - 1:page]
    carry_ref[...] = carry

    m_old = m_ref[...]
    m_blk = s_list[0]
    for s in s_list[1:]:
        m_blk = jnp.maximum(m_blk, s)
    m_new = jnp.maximum(m_old, jnp.max(m_blk, axis=1, keepdims=True))
    alpha = jnp.exp(m_old - m_new)
    p_sum = jnp.zeros((N_HEADS, page), F32)
    acc = alpha * acc_ref[...]
    for i in range(pp):
        p = jnp.exp(s_list[i] - m_new)
        p_sum = p_sum + p
        acc = acc + lax.dot_general(p.astype(BF16), v_refs[i][...].astype(BF16), (((1,), (1,)), ((), ())),
                                    preferred_element_type=F32)
    l_new = alpha * l_ref[...] + jnp.sum(p_sum, axis=1, keepdims=True)
    m_ref[...] = m_new
    l_ref[...] = l_new
    acc_ref[...] = acc

    @pl.when(j == nj - 1)
    def _():
        kn = jnp.broadcast_to(kn_ref[0], (N_HEADS, D_ATT)).astype(BF16).astype(F32)
        vn = jnp.broadcast_to(vn_ref[0], (N_HEADS, D_ATT)).astype(BF16).astype(F32)
        fn = fn_ref[0]
        s_new = jnp.sum(qbd32 * kn, axis=1, keepdims=True) - (carry + fn)
        m_fin = jnp.maximum(m_new, s_new)
        a_fin = jnp.exp(m_new - m_fin)
        p_new = jnp.exp(s_new - m_fin)
        l_fin = a_fin * l_new + p_new
        acc_fin = a_fin * acc + p_new.astype(BF16).astype(F32) * vn
        o = jnp.where(own, acc_fin / l_fin, 0.0)
        o_ref[0] = jnp.sum(o, axis=0, keepdims=True).astype(BF16)


def _attn_sample(page_table, q, k_new, v_new, logf_new, tri, cache_k, cache_v, cache_f, layer, pp):
    bd, n_pages = page_table.shape
    page = cache_k.shape[3]
    nj = n_pages // pp
    tok = lambda w: pl.BlockSpec((1, 1, w), lambda b, j, pt: (b, 0, 0))
    kv_spec = lambda i: pl.BlockSpec((None, None, D_ATT, page),
                                     lambda b, j, pt, i=i: (layer, pt[b, j * pp + i], 0, 0))
    f_spec = lambda i: pl.BlockSpec((None, None, N_HEADS, page),
                                    lambda b, j, pt, i=i: (layer, pt[b, j * pp + i], 0, 0))
    in_specs = ([tok(D_ATT), tok(D_ATT), tok(D_ATT), pl.BlockSpec((1, N_HEADS, 1), lambda b, j, pt: (b, 0, 0)),
                 pl.BlockSpec((page, page), lambda b, j, pt: (0, 0))]
                + [kv_spec(i) for i in range(pp)] + [kv_spec(i) for i in range(pp)] + [f_spec(i) for i in range(pp)])
    grid_spec = pltpu.PrefetchScalarGridSpec(
        num_scalar_prefetch=1, grid=(bd, nj), in_specs=in_specs,
        out_specs=pl.BlockSpec((1, 1, D_ATT), lambda b, j, pt: (b, 0, 0)),
        scratch_shapes=[pltpu.VMEM((N_HEADS, 1), F32), pltpu.VMEM((N_HEADS, 1), F32),
                        pltpu.VMEM((N_HEADS, D_ATT), F32), pltpu.VMEM((N_HEADS, 1), F32)])
    out = pl.pallas_call(
        functools.partial(_attn_sample_kernel, pp=pp, page=page),
        grid_spec=grid_spec, out_shape=jax.ShapeDtypeStruct((bd, 1, D_ATT), BF16),
        compiler_params=_cparams(2), name="attn_sample",
    )(page_table, q[:, None, :], k_new[:, None, :], v_new[:, None, :], logf_new[:, :N_HEADS, None], tri,
      *([cache_k] * pp), *([cache_v] * pp), *([cache_f] * pp))
    return out[:, 0, :]


def _layer_weights(l, tm, norm1_g, w_in, b_f, b_gate, q_gain, k_gain, pool_w, pool_scale, w_pool_up, w_att_up,
                   dw_w, dw_b, conv_ln_g, conv_ln_b, w_conv_out, w_out, norm2_g, w_mlp_up, w_mlp_down):
    d = w_in.shape[1]
    w = w_in[l]
    o_f = 3 * D_ATT
    o_pool = o_f + N_HEADS
    o_conv = o_pool + 256
    o_gate = o_conv + 512
    w_main = jnp.concatenate([w[:, :o_f], w[:, o_pool:o_gate], w[:, o_f:o_pool],
                              jnp.zeros((d, LANES - N_HEADS), F32)], axis=1).astype(BF16)
    grp = jnp.arange(256) // HEAD_DIM
    poolw = jnp.zeros((256, 256), F32)
    for g in range(4):
        poolw = poolw.at[g * 64:(g + 1) * 64, g * 64:(g + 1) * 64].set(pool_w[l, g])
    return dict(
        n1=norm1_g[l][None, :], w_main=w_main, w_gate=w[:, o_gate:].astype(BF16),
        bf=jnp.pad(b_f[l], (0, LANES - N_HEADS))[None, :], bg=b_gate[l],
        qg=jnp.tile(q_gain[l], N_HEADS)[None, :], kg=jnp.tile(k_gain[l], N_HEADS)[None, :],
        gsum=(grp[:, None] == grp[None, :]).astype(BF16),
        ltri=(jnp.arange(tm)[:, None] >= jnp.arange(tm)[None, :]).astype(BF16),
        poolw=poolw.astype(BF16), pscale=pool_scale[l][None, :],
        dww=jnp.pad(dw_w[l], ((0, CONV_HIST - CONV_WIDTH), (0, 0))), dwb=dw_b[l][None, :],
        lng=conv_ln_g[l][None, :], lnb=conv_ln_b[l][None, :],
        w_att_up=w_att_up[l].astype(BF16), w_pool_up=w_pool_up[l].astype(BF16),
        w_conv_out=w_conv_out[l].astype(BF16), w_out=w_out[l].astype(BF16),
        n2=norm2_g[l][None, :], w_up=w_mlp_up[l].astype(BF16), w_down=w_mlp_down[l].astype(BF16))


def kernel(x_prompt, x_sample, cache_k, cache_v, cache_logf, state_pool, state_conv, page_table,
           norm1_g, w_in, b_f, b_gate, q_gain, k_gain, pool_w, pool_scale, w_pool_up, w_att_up,
           dw_w, dw_b, conv_ln_g, conv_ln_b, w_conv_out, w_out, norm2_g, w_mlp_up, w_mlp_down):
    B, T, D = x_prompt.shape
    bd = x_sample.shape[0]
    depth = w_in.shape[0]
    n_pool_pages, page = cache_k.shape[1], cache_k.shape[2]
    n_pages = page_table.shape[1]
    past_len = n_pages * page
    tm = min(512, T)
    pp = min(32, n_pages)

    tri = (jnp.arange(page)[:, None] <= jnp.arange(page)[None, :]).astype(BF16)
    ck_flat = jnp.transpose(cache_k, (0, 1, 3, 4, 2)).reshape(depth, n_pool_pages, D_ATT, page)
    cv_flat = jnp.transpose(cache_v, (0, 1, 3, 4, 2)).reshape(depth, n_pool_pages, D_ATT, page)
    cf_flat = jnp.transpose(cache_logf, (0, 1, 3, 2))

    xp = x_prompt
    xs = x_sample.reshape(bd, D)
    outs = [[] for _ in range(7)]
    stacks = ()
    for l in range(depth):
        lw = _layer_weights(l, tm, norm1_g, w_in, b_f, b_gate, q_gain, k_gain, pool_w, pool_scale, w_pool_up,
                            w_att_up, dw_w, dw_b, conv_ln_g, conv_ln_b, w_conv_out, w_out, norm2_g,
                            w_mlp_up, w_mlp_down)
        qt, ka, vtb, kt_all, vt_all, ft_all, pm, ca, ptail, ctail = _inproj_prompt(xp, lw, tm, l, depth, stacks)
        stacks = (kt_all, vt_all, ft_all)
        att_t = _attn_prompt(qt, ka, vtb, tm)
        x1 = _merge(xp, att_t, pm, ca, lw, tm)
        xp = _mlp(x1.reshape(B * T, D), lw, tm).reshape(B, T, D)
        sp_t = jnp.transpose(state_pool[l], (1, 0, 2))
        sc_t = jnp.transpose(state_conv[l], (1, 0, 2))
        qs, ks, vs, fs, us, gs, pms, cas = _inproj_sample(xs, lw, sp_t, sc_t, past_len)
        att_s = _attn_sample(page_table, qs, ks, vs, fs, tri, ck_flat, cv_flat, cf_flat, l, pp)
        xs1 = _merge(xs[None], att_s.T[None], pms[None], cas[None], lw, bd)[0]
        xs = _mlp(xs1, lw, bd)

        outs[0].append(ptail[:, POOL_HIST - POOL_BUF:])
        outs[1].append(ctail[:, CONV_HIST - CONV_BUF:])
        outs[2].append(ks.reshape(bd, 1, N_HEADS, HEAD_DIM))
        outs[3].append(vs.reshape(bd, 1, N_HEADS, HEAD_DIM))
        outs[4].append(fs[:, None, :N_HEADS])
        outs[5].append(jnp.concatenate([state_pool[l][:, 1:], us[:, None, :]], axis=1))
        outs[6].append(jnp.concatenate([state_conv[l][:, 1:], gs[:, None, :]], axis=1))
    kt_all, vt_all, ft_all = stacks
    k_prompt = jnp.transpose(kt_all.reshape(depth, B, N_HEADS, HEAD_DIM, T), (0, 1, 4, 2, 3))
    v_prompt = jnp.transpose(vt_all.reshape(depth, B, N_HEADS, HEAD_DIM, T), (0, 1, 4, 2, 3))
    f_prompt = jnp.transpose(ft_all, (0, 1, 3, 2))
    st = [jnp.stack(o) for o in outs]
    return (xp, xs.reshape(bd, 1, D), k_prompt, v_prompt, f_prompt) + tuple(st)
```

```python
import functools

import jax
import jax.numpy as jnp
from jax import lax
from jax.experimental import pallas as pl
from jax.experimental.pallas import tpu as pltpu

F32 = jnp.float32
BF16 = jnp.bfloat16

N_HEADS = 8
HEAD_DIM = 64
D_ATT = N_HEADS * HEAD_DIM
ATT_HEADS = 8
POOL_WINDOWS = (2, 4, 8, 16)
POOL_BUF = max(POOL_WINDOWS) - 1
POOL_HIST = 32
CONV_WIDTH = 31
CONV_BUF = CONV_WIDTH - 1
CONV_HIST = 32
NORM_EPS = 1e-6
LN_EPS = 1e-5
LOG2E = 1.4426950408889634
N_SUB = 2
S_CHUNK = 512
LANES = 128
VMEM_LIMIT = 56 * 1024 * 1024

C_Q, C_K, C_V, C_POOL, C_CONV, C_F = 0, 512, 1024, 1536, 1792, 2304
D_MAIN = C_F + LANES


def _cparams(n_axes):
    return pltpu.CompilerParams(dimension_semantics=("arbitrary",) * n_axes, vmem_limit_bytes=VMEM_LIMIT)


def _split3(x):
    hi = x.astype(BF16).astype(F32)
    r = x - hi
    mid = r.astype(BF16).astype(F32)
    lo = (r - mid).astype(BF16).astype(F32)
    return hi, mid, lo


def _rms(x, g):
    return x * lax.rsqrt(jnp.mean(x * x, axis=-1, keepdims=True) + NORM_EPS) * g


def _log_sigmoid(x):
    return jnp.minimum(x, 0.0) - jnp.log1p(jnp.exp(-jnp.abs(x)))


def _head_norm(a, gsum, gain):
    outs = []
    for c in range(2):
        blk = a[:, c * 256:(c + 1) * 256]
        sq = blk * blk
        hi = sq.astype(BF16)
        lo = (sq - hi.astype(F32)).astype(BF16)
        ss = jnp.dot(hi, gsum, preferred_element_type=F32) + jnp.dot(lo, gsum, preferred_element_type=F32)
        outs.append(blk * lax.rsqrt(ss * (1.0 / HEAD_DIM) + NORM_EPS))
    return jnp.concatenate(outs, axis=1) * gain


def _pool_window_lanes(shape, col_block):
    lane = lax.broadcasted_iota(jnp.int32, shape, len(shape) - 1)
    lo_w, hi_w = POOL_WINDOWS[2 * col_block], POOL_WINDOWS[2 * col_block + 1]
    return jnp.where(lane < 64, lo_w, hi_w)


def _layer_norm_silu(y, g, b):
    mu = jnp.mean(y, axis=-1, keepdims=True)
    d = y - mu
    var = jnp.mean(d * d, axis=-1, keepdims=True)
    yn = d * lax.rsqrt(var + LN_EPS) * g + b
    return yn * jax.nn.sigmoid(yn)


def _inproj_prompt_kernel(x_ref, n1_ref, w_ref, bf_ref, qg_ref, kg_ref, gsum_ref, ltri_ref, poolw_ref,
                          pscale_ref, dww_ref, dwb_ref, lng_ref, lnb_ref,
                          *rest, tm, n_prev):
    (qt_ref, ka_ref, vtb_ref, kt_ref, vt_ref, logft_ref, pm_ref, ca_ref, ptail_ref, ctail_ref,
     uh_ref, gh_ref, sh_ref, carry_ref) = rest[n_prev:]
    t = pl.program_id(1)

    @pl.when(t == 0)
    def _():
        uh_ref[0:POOL_HIST, :] = jnp.zeros((POOL_HIST, 256), F32)
        gh_ref[0:CONV_HIST, :] = jnp.zeros((CONV_HIST, 256), F32)
        carry_ref[...] = jnp.zeros((1, LANES), F32)

    sub = tm // N_SUB
    rows = [slice(i * sub, (i + 1) * sub) for i in range(N_SUB)]
    zs = [jnp.dot(_rms(x_ref[0, rs, :], n1_ref[...]).astype(BF16), w_ref[...], preferred_element_type=F32)
          for rs in rows]
    lane = lax.broadcasted_iota(jnp.int32, (sub, LANES), 1)
    row8 = lax.broadcasted_iota(jnp.int32, (8, sub), 0)
    pad_rows = jnp.zeros((HEAD_DIM - 8, sub), F32)
    first = CONV_HIST - CONV_BUF

    for rs, z in zip(rows, zs):
        r0 = rs.start
        def head_norm_t(zt):
            parts = []
            for hd in range(N_HEADS):
                blk = zt[hd * HEAD_DIM:(hd + 1) * HEAD_DIM, :]
                parts.append(blk * lax.rsqrt(jnp.mean(blk * blk, axis=0, keepdims=True) + NORM_EPS))
            return jnp.concatenate(parts, axis=0)

        qt = head_norm_t(z[:, C_Q:C_Q + D_ATT].T) * qg_ref[...] * (HEAD_DIM ** -0.5 * LOG2E)
        knt = head_norm_t(z[:, C_K:C_K + D_ATT].T) * kg_ref[...]
        kn = knt.T
        vt = z[:, C_V:C_V + D_ATT].T
        kt_ref[0, :, rs] = knt
        vt_ref[0, :, rs] = vt

        logf = jnp.where(lane < N_HEADS, _log_sigmoid(z[:, C_F:C_F + LANES] + bf_ref[...]), 0.0)
        logft_ref[0, :, rs] = logf.T[0:N_HEADS, :]
        f_hi, f_mid, f_lo = _split3(logf)
        f3 = (f_hi + pltpu.roll(f_mid, 8, 1) + pltpu.roll(f_lo, 16, 1)).astype(BF16)
        cs3 = jnp.dot(ltri_ref[0:sub, 0:sub], f3, preferred_element_type=F32)
        c = cs3 + pltpu.roll(cs3, LANES - 8, 1) + pltpu.roll(cs3, LANES - 16, 1) + carry_ref[...]
        c = jnp.where(lane < N_HEADS, c, 0.0)
        carry_ref[...] = c[sub - 1:sub, :]
        c2 = c * LOG2E
        c_hi, c_mid, c_lo = _split3(c2)
        ct_hi, ct_mid, ct_lo = _split3(c2.T[0:N_HEADS, :])

        for hd in range(N_HEADS):
            p, odd = hd // 2, hd % 2
            base = 0 if odd else 64
            ch, cm, cl = c_hi[:, hd:hd + 1], c_mid[:, hd:hd + 1], c_lo[:, hd:hd + 1]
            ones_k = (lane >= base) & (lane < base + 3)
            ek = jnp.where(lane == base + 3, -ch, jnp.where(lane == base + 4, -cm, jnp.where(lane == base + 5, -cl,
                           jnp.where(ones_k, 1.0, 0.0))))
            val = (lane >= 64) if odd else (lane < 64)
            ka_ref[0, hd, rs, :] = jnp.where(val, kn[:, p * LANES:(p + 1) * LANES], ek).astype(BF16)
            eq = jnp.where(row8 == 0, ct_hi[hd:hd + 1, :], jnp.where(row8 == 1, ct_mid[hd:hd + 1, :],
                           jnp.where(row8 == 2, ct_lo[hd:hd + 1, :], jnp.where(row8 < 6, 1.0, 0.0))))
            qh = qt[hd * HEAD_DIM:(hd + 1) * HEAD_DIM, :]
            parts = [eq, pad_rows, qh] if odd else [qh, eq, pad_rows]
            qt_ref[0, hd, :, rs] = jnp.concatenate(parts, axis=0).astype(BF16)
            vtb_ref[0, hd, 0, :, rs] = vt[hd * HEAD_DIM:(hd + 1) * HEAD_DIM, :].astype(BF16)

        u = z[:, C_POOL:C_POOL + 256]
        uh_ref[POOL_HIST + r0:POOL_HIST + r0 + sub, :] = u
        pos = t * tm + r0 + lax.broadcasted_iota(jnp.int32, (sub, LANES), 0)
        n_ext = POOL_HIST + sub
        mixed = []
        for cb in range(2):
            wl = _pool_window_lanes((sub, LANES), cb)
            e = uh_ref[r0:r0 + n_ext, cb * LANES:(cb + 1) * LANES]
            s2 = e[8:n_ext] + e[7:n_ext - 1]
            s4 = s2[8:] + s2[6:n_ext - 10]
            if cb == 0:
                lo, hi = s2[24:24 + sub], s4[16:16 + sub]
            else:
                s8 = s4[8:] + s4[4:n_ext - 20]
                s16 = s8[8:] + s8[0:n_ext - 32]
                lo, hi = s8[8:8 + sub], s16
            acc = jnp.where(lane < 64, lo, hi)
            cnt = jnp.minimum(pos + 1, wl).astype(F32)
            mixed.append(acc / cnt - u[:, cb * LANES:(cb + 1) * LANES])
        mixed = jnp.concatenate(mixed, axis=1).astype(BF16)
        pm = jnp.dot(mixed, poolw_ref[...], preferred_element_type=F32) * pscale_ref[...]
        pm_ref[0, rs, :] = pm.astype(BF16)

        glu = z[:, C_CONV:C_CONV + 256] * jax.nn.sigmoid(z[:, C_CONV + 256:C_CONV + 512])
        gh_ref[CONV_HIST + r0:CONV_HIST + r0 + sub, :] = glu
        y = jnp.zeros((sub, 256), F32) + dwb_ref[...]
        for r in range(8):
            offs = [o for o in range(first, first + CONV_WIDTH) if o % 8 == r]
            a_max = max(offs) // 8
            if r:
                sh_ref[0:sub + 8 * a_max, :] = gh_ref[r0 + r:r0 + r + sub + 8 * a_max, :]
            for o in offs:
                a = o // 8
                tap = sh_ref[8 * a:8 * a + sub, :] if r else gh_ref[r0 + 8 * a:r0 + 8 * a + sub, :]
                y = y + dww_ref[o - first:o - first + 1, :] * tap
        ca_ref[0, rs, :] = _layer_norm_silu(y, lng_ref[...], lnb_ref[...]).astype(BF16)

    ptail_ref[0] = uh_ref[tm:tm + POOL_HIST, :]
    uh_ref[0:POOL_HIST, :] = uh_ref[tm:tm + POOL_HIST, :]
    ctail_ref[0] = gh_ref[tm:tm + CONV_HIST, :]
    gh_ref[0:CONV_HIST, :] = gh_ref[tm:tm + CONV_HIST, :]


def _inproj_prompt(x, lw, tm, layer, depth, prev):
    B, T, D = x.shape
    nt = T // tm
    const = lambda shape: pl.BlockSpec(shape, lambda b, t: (0,) * len(shape))
    row = lambda w: pl.BlockSpec((1, tm, w), lambda b, t: (b, t, 0))
    tail = lambda r: pl.BlockSpec((1, r, 256), lambda b, t: (b, 0, 0))
    stack = lambda r: pl.BlockSpec((None, 1, r, tm), lambda b, t: (layer, b, 0, t))
    out_shape = (
        jax.ShapeDtypeStruct((B, N_HEADS, LANES, T), BF16),
        jax.ShapeDtypeStruct((B, N_HEADS, T, LANES), BF16),
        jax.ShapeDtypeStruct((B, N_HEADS, nt, HEAD_DIM, tm), BF16),
        jax.ShapeDtypeStruct((depth, B, D_ATT, T), F32),
        jax.ShapeDtypeStruct((depth, B, D_ATT, T), F32),
        jax.ShapeDtypeStruct((depth, B, N_HEADS, T), F32),
        jax.ShapeDtypeStruct((B, T, 256), BF16),
        jax.ShapeDtypeStruct((B, T, 256), BF16),
        jax.ShapeDtypeStruct((B, POOL_HIST, 256), F32),
        jax.ShapeDtypeStruct((B, CONV_HIST, 256), F32),
    )
    out_specs = (pl.BlockSpec((1, N_HEADS, LANES, tm), lambda b, t: (b, 0, 0, t)),
                 pl.BlockSpec((1, N_HEADS, tm, LANES), lambda b, t: (b, 0, t, 0)),
                 pl.BlockSpec((1, N_HEADS, 1, HEAD_DIM, tm), lambda b, t: (b, 0, t, 0, 0)),
                 stack(D_ATT), stack(D_ATT), stack(N_HEADS), row(256), row(256), tail(POOL_HIST), tail(CONV_HIST))
    in_specs = [row(D), const((1, D)), const((D, D_MAIN)), const((1, LANES)), const((D_ATT, 1)), const((D_ATT, 1)),
                const((256, 256)), const((tm, tm)), const((256, 256)), const((1, 256)), const((CONV_HIST, 256)),
                const((1, 256)), const((1, 256)), const((1, 256))]
    n_in = len(in_specs)
    in_specs += [pl.BlockSpec(memory_space=pl.ANY)] * len(prev)
    aliases = {n_in + i: 3 + i for i in range(len(prev))}
    return pl.pallas_call(
        functools.partial(_inproj_prompt_kernel, tm=tm, n_prev=len(prev)),
        grid=(B, nt), in_specs=in_specs, out_specs=out_specs, out_shape=out_shape,
        scratch_shapes=[pltpu.VMEM((POOL_HIST + tm, 256), F32), pltpu.VMEM((CONV_HIST + tm, 256), F32),
                        pltpu.VMEM((CONV_HIST + tm, 256), F32), pltpu.VMEM((1, LANES), F32)],
        input_output_aliases=aliases,
        compiler_params=_cparams(2), name="inproj_prompt",
    )(x, lw["n1"], lw["w_main"], lw["bf"], lw["qg_col"], lw["kg_col"], lw["gsum"], lw["ltri"], lw["poolw"],
      lw["pscale"], lw["dww"], lw["dwb"], lw["lng"], lw["lnb"], *prev)


def _attn_prompt_kernel(qt_ref, ka_ref, vt_ref, o_ref, s_ref, p_ref, m_ref, l_ref, acc_ref, mx_ref, *, tq):
    i = pl.program_id(2)
    n_chunk = tq // S_CHUNK
    for hh in range(ATT_HEADS):
        m_ref[hh] = jnp.full((1, tq), -jnp.inf, F32)
        l_ref[hh] = jnp.zeros((1, tq), F32)
        acc_ref[hh] = jnp.zeros((HEAD_DIM, tq), F32)

    def score(hh, j, masked):
        start = pl.multiple_of(j * tq, tq)
        for r in range(n_chunk):
            k = ka_ref[0, hh, pl.ds(start + r * S_CHUNK, S_CHUNK), :]
            st = jnp.dot(k, qt_ref[0, hh], preferred_element_type=F32)
            if masked:
                key = lax.broadcasted_iota(jnp.int32, (S_CHUNK, tq), 0) + r * S_CHUNK
                qry = lax.broadcasted_iota(jnp.int32, (S_CHUNK, tq), 1)
                st = jnp.where(key <= qry, st, -jnp.inf)
            s_ref[hh, r * S_CHUNK:(r + 1) * S_CHUNK, :] = st
            cm = jnp.max(st, axis=0, keepdims=True)
            mx = cm if r == 0 else jnp.maximum(mx, cm)
        mx_ref[hh] = mx

    def softmax(hh):
        m_old = m_ref[hh]
        m_new = jnp.maximum(m_old, mx_ref[hh])
        p = jnp.exp2(s_ref[hh] - m_new)
        alpha = jnp.exp2(m_old - m_new)
        l_ref[hh] = alpha * l_ref[hh] + jnp.sum(p, axis=0, keepdims=True)
        m_ref[hh] = m_new
        p_ref[hh] = p.astype(BF16)
        return alpha

    def pv(hh, jv, alpha):
        acc_ref[hh] = alpha * acc_ref[hh] + jnp.dot(vt_ref[0, hh, jv], p_ref[hh],
                                                   preferred_element_type=F32)

    for hh in range(ATT_HEADS):
        score(hh, i, True)

    def body(t, carry):
        prev = jnp.where(t == 0, i, t - 1)
        for hh in range(ATT_HEADS):
            alpha = softmax(hh)
            score(hh, t, False)
            pv(hh, prev, alpha)
        return carry

    lax.fori_loop(0, i, body, 0)
    last = jnp.maximum(i - 1, 0)
    for hh in range(ATT_HEADS):
        pv(hh, last, softmax(hh))
    o = jnp.concatenate([acc_ref[hh] / l_ref[hh] for hh in range(ATT_HEADS)], axis=0)
    o_ref[0] = o.astype(BF16)


def _attn_prompt(qt, ka, vtb, tq):
    B, _, _, T = qt.shape
    nq = T // tq
    nh = ATT_HEADS
    return pl.pallas_call(
        functools.partial(_attn_prompt_kernel, tq=tq),
        grid=(B, N_HEADS // nh, nq),
        in_specs=[pl.BlockSpec((1, nh, LANES, tq), lambda b, p, i: (b, p, 0, i)),
                  pl.BlockSpec((1, nh, T, LANES), lambda b, p, i: (b, p, 0, 0), pipeline_mode=pl.Buffered(1)),
                  pl.BlockSpec((1, nh, nq, HEAD_DIM, tq), lambda b, p, i: (b, p, 0, 0, 0),
                               pipeline_mode=pl.Buffered(1))],
        out_specs=pl.BlockSpec((1, nh * HEAD_DIM, tq), lambda b, p, i: (b, p, i)),
        out_shape=jax.ShapeDtypeStruct((B, D_ATT, T), BF16),
        scratch_shapes=[pltpu.VMEM((nh, tq, tq), F32), pltpu.VMEM((nh, tq, tq), BF16),
                        pltpu.VMEM((nh, 1, tq), F32), pltpu.VMEM((nh, 1, tq), F32),
                        pltpu.VMEM((nh, HEAD_DIM, tq), F32), pltpu.VMEM((nh, 1, tq), F32)],
        compiler_params=_cparams(3), name="attn_prompt",
    )(qt, ka, vtb)


def _merge_kernel(x_ref, att_ref, pm_ref, ca_ref, n1_ref, wg_ref, bg_ref, wau_ref, wpu_ref, wco_ref, wo_ref, o_ref):
    x = x_ref[0]
    d = x.shape[1]
    h = _rms(x, n1_ref[...]).astype(BF16)
    branches = (lax.dot_general(att_ref[0], wau_ref[...], (((0,), (0,)), ((), ())), preferred_element_type=F32),
                jnp.dot(pm_ref[0], wpu_ref[...], preferred_element_type=F32),
                jnp.dot(ca_ref[0], wco_ref[...], preferred_element_type=F32))
    merged = jnp.zeros_like(x)
    for br in range(3):
        gl = jnp.dot(h, wg_ref[:, br * d:(br + 1) * d], preferred_element_type=F32) + bg_ref[br:br + 1, :]
        merged = merged + jax.nn.sigmoid(gl) * branches[br]
    o_ref[0] = x + jnp.dot(merged.astype(BF16), wo_ref[...], preferred_element_type=F32)


def _merge(x, att_t, pm, ca, lw, tm):
    B, T, d = x.shape
    const = lambda shape: pl.BlockSpec(shape, lambda b, t: (0, 0))
    row = lambda w: pl.BlockSpec((1, tm, w), lambda b, t: (b, t, 0))
    return pl.pallas_call(
        _merge_kernel, grid=(B, T // tm),
        in_specs=[row(d), pl.BlockSpec((1, D_ATT, tm), lambda b, t: (b, 0, t)), row(256), row(256),
                  const((1, d)), const((d, 3 * d)), const((3, d)),
                  const((D_ATT, d)), const((256, d)), const((256, d)), const((d, d))],
        out_specs=row(d), out_shape=jax.ShapeDtypeStruct((B, T, d), F32),
        compiler_params=_cparams(2), name="merge",
    )(x, att_t, pm, ca, lw["n1"], lw["w_gate"], lw["bg"], lw["w_att_up"], lw["w_pool_up"], lw["w_conv_out"], lw["w_out"])


def _mlp_kernel(x_ref, n2_ref, wu_ref, wd_ref, o_ref):
    x = x_ref[...]
    h = _rms(x, n2_ref[...]).astype(BF16)
    a = jnp.maximum(jnp.dot(h, wu_ref[...], preferred_element_type=F32), 0.0)
    hid = (a * a).astype(BF16)
    o_ref[...] = x + jnp.dot(hid, wd_ref[...], preferred_element_type=F32)


def _mlp(x, lw, tm):
    n, d = x.shape
    dff = lw["w_up"].shape[1]
    single = pl.Buffered(1)
    return pl.pallas_call(
        _mlp_kernel, grid=(n // tm,),
        in_specs=[pl.BlockSpec((tm, d), lambda t: (t, 0)), pl.BlockSpec((1, d), lambda t: (0, 0)),
                  pl.BlockSpec((d, dff), lambda t: (0, 0), pipeline_mode=single),
                  pl.BlockSpec((dff, d), lambda t: (0, 0), pipeline_mode=single)],
        out_specs=pl.BlockSpec((tm, d), lambda t: (t, 0)), out_shape=jax.ShapeDtypeStruct((n, d), F32),
        compiler_params=_cparams(1), name="mlp",
    )(x, lw["n2"], lw["w_up"], lw["w_down"])


def _inproj_sample_kernel(x_ref, n1_ref, w_ref, bf_ref, qg_ref, kg_ref, gsum_ref, poolw_ref, pscale_ref,
                          dww_ref, dwb_ref, lng_ref, lnb_ref, sp_ref, sc_ref,
                          q_ref, k_ref, v_ref, logf_ref, u_ref, glu_ref, pm_ref, ca_ref, *, start_pos):
    x = x_ref[...]
    n = x.shape[0]
    h = _rms(x, n1_ref[...]).astype(BF16)
    z = jnp.dot(h, w_ref[...], preferred_element_type=F32)
    gsum = gsum_ref[...]
    q_ref[...] = (_head_norm(z[:, C_Q:C_Q + D_ATT], gsum, qg_ref[...]) * (HEAD_DIM ** -0.5)).astype(BF16)
    k_ref[...] = _head_norm(z[:, C_K:C_K + D_ATT], gsum, kg_ref[...])
    v_ref[...] = z[:, C_V:C_V + D_ATT]
    logf_ref[...] = _log_sigmoid(z[:, C_F:C_F + LANES] + bf_ref[...])

    u = z[:, C_POOL:C_POOL + 256]
    u_ref[...] = u
    mixed = []
    for cb in range(2):
        wl = _pool_window_lanes((n, LANES), cb)
        acc = u[:, cb * LANES:(cb + 1) * LANES]
        for j in range(1, POOL_WINDOWS[2 * cb + 1]):
            acc = acc + jnp.where(wl > j, sp_ref[POOL_BUF - j, :, cb * LANES:(cb + 1) * LANES], 0.0)
        cnt = jnp.minimum(start_pos + 1, wl).astype(F32)
        mixed.append(acc / cnt - u[:, cb * LANES:(cb + 1) * LANES])
    mixed = jnp.concatenate(mixed, axis=1).astype(BF16)
    pm_ref[...] = (jnp.dot(mixed, poolw_ref[...], preferred_element_type=F32) * pscale_ref[...]).astype(BF16)

    glu = z[:, C_CONV:C_CONV + 256] * jax.nn.sigmoid(z[:, C_CONV + 256:C_CONV + 512])
    glu_ref[...] = glu
    y = dwb_ref[...] + dww_ref[CONV_BUF:CONV_BUF + 1, :] * glu
    for j in range(CONV_BUF):
        y = y + dww_ref[j:j + 1, :] * sc_ref[j]
    ca_ref[...] = _layer_norm_silu(y, lng_ref[...], lnb_ref[...]).astype(BF16)


def _inproj_sample(x, lw, sp_t, sc_t, start_pos):
    n, d = x.shape
    sds = jax.ShapeDtypeStruct
    out_shape = (sds((n, D_ATT), BF16), sds((n, D_ATT), F32), sds((n, D_ATT), F32), sds((n, LANES), F32),
                 sds((n, 256), F32), sds((n, 256), F32), sds((n, 256), BF16), sds((n, 256), BF16))
    return pl.pallas_call(
        functools.partial(_inproj_sample_kernel, start_pos=start_pos),
        out_shape=out_shape,
        compiler_params=pltpu.CompilerParams(vmem_limit_bytes=VMEM_LIMIT), name="inproj_sample",
    )(x, lw["n1"], lw["w_main"], lw["bf"], lw["qg"], lw["kg"], lw["gsum"], lw["poolw"], lw["pscale"],
      lw["dww"], lw["dwb"], lw["lng"], lw["lnb"], sp_t, sc_t)


def _attn_sample_kernel(pt_ref, q_ref, kn_ref, vn_ref, fn_ref, tri_ref, *refs, pp, page):
    k_refs, v_refs, f_refs = refs[:pp], refs[pp:2 * pp], refs[2 * pp:3 * pp]
    o_ref = refs[3 * pp]
    m_ref, l_ref, acc_ref, carry_ref = refs[3 * pp + 1:]
    j = pl.program_id(1)
    nj = pl.num_programs(1)

    @pl.when(j == 0)
    def _():
        m_ref[...] = jnp.full((N_HEADS, 1), -jnp.inf, F32)
        l_ref[...] = jnp.zeros((N_HEADS, 1), F32)
        acc_ref[...] = jnp.zeros((N_HEADS, D_ATT), F32)
        carry_ref[...] = jnp.zeros((N_HEADS, 1), F32)

    head_of_lane = lax.broadcasted_iota(jnp.int32, (N_HEADS, D_ATT), 1) // HEAD_DIM
    own = head_of_lane == lax.broadcasted_iota(jnp.int32, (N_HEADS, D_ATT), 0)
    qbd32 = jnp.where(own, jnp.broadcast_to(q_ref[0].astype(F32), (N_HEADS, D_ATT)), 0.0)
    qbd = qbd32.astype(BF16)

    f_hi, f_mid, f_lo = _split3(jnp.concatenate([f_refs[i][...] for i in range(pp)], axis=0))
    tri = tri_ref[...]
    cs_all = (jnp.dot(f_hi.astype(BF16), tri, preferred_element_type=F32)
              + jnp.dot(f_mid.astype(BF16), tri, preferred_element_type=F32)
              + jnp.dot(f_lo.astype(BF16), tri, preferred_element_type=F32))

    s_list = []
    carry = carry_ref[...]
    for i in range(pp):
        kt = k_refs[i][...].astype(BF16)
        s = jnp.dot(qbd, kt, preferred_element_type=F32)
        cs = cs_all[i * N_HEADS:(i + 1) * N_HEADS, :]
        s_list.append(s - (cs + carry))
        carry = carry + cs[:, page - 1:page]
    carry_ref[...] = carry

    m_old = m_ref[...]
    m_blk = s_list[0]
    for s in s_list[1:]:
        m_blk = jnp.maximum(m_blk, s)
    m_new = jnp.maximum(m_old, jnp.max(m_blk, axis=1, keepdims=True))
    alpha = jnp.exp(m_old - m_new)
    p_sum = jnp.zeros((N_HEADS, page), F32)
    acc = alpha * acc_ref[...]
    for i in range(pp):
        p = jnp.exp(s_list[i] - m_new)
        p_sum = p_sum + p
        acc = acc + lax.dot_general(p.astype(BF16), v_refs[i][...].astype(BF16), (((1,), (1,)), ((), ())),
                                    preferred_element_type=F32)
    l_new = alpha * l_ref[...] + jnp.sum(p_sum, axis=1, keepdims=True)
    m_ref[...] = m_new
    l_ref[...] = l_new
    acc_ref[...] = acc

    @pl.when(j == nj - 1)
    def _():
        kn = jnp.broadcast_to(kn_ref[0], (N_HEADS, D_ATT)).astype(BF16).astype(F32)
        vn = jnp.broadcast_to(vn_ref[0], (N_HEADS, D_ATT)).astype(BF16).astype(F32)
        fn = fn_ref[0]
        s_new = jnp.sum(qbd32 * kn, axis=1, keepdims=True) - (carry + fn)
        m_fin = jnp.maximum(m_new, s_new)
        a_fin = jnp.exp(m_new - m_fin)
        p_new = jnp.exp(s_new - m_fin)
        l_fin = a_fin * l_new + p_new
        acc_fin = a_fin * acc + p_new.astype(BF16).astype(F32) * vn
        o = jnp.where(own, acc_fin / l_fin, 0.0)
        o_ref[0] = jnp.sum(o, axis=0, keepdims=True).astype(BF16)


def _attn_sample(page_table, q, k_new, v_new, logf_new, tri, cache_k, cache_v, cache_f, layer, pp):
    bd, n_pages = page_table.shape
    page = cache_k.shape[3]
    nj = n_pages // pp
    tok = lambda w: pl.BlockSpec((1, 1, w), lambda b, j, pt: (b, 0, 0))
    kv_spec = lambda i: pl.BlockSpec((None, None, D_ATT, page),
                                     lambda b, j, pt, i=i: (layer, pt[b, j * pp + i], 0, 0))
    f_spec = lambda i: pl.BlockSpec((None, None, N_HEADS, page),
                                    lambda b, j, pt, i=i: (layer, pt[b, j * pp + i], 0, 0))
    in_specs = ([tok(D_ATT), tok(D_ATT), tok(D_ATT), pl.BlockSpec((1, N_HEADS, 1), lambda b, j, pt: (b, 0, 0)),
                 pl.BlockSpec((page, page), lambda b, j, pt: (0, 0))]
                + [kv_spec(i) for i in range(pp)] + [kv_spec(i) for i in range(pp)] + [f_spec(i) for i in range(pp)])
    grid_spec = pltpu.PrefetchScalarGridSpec(
        num_scalar_prefetch=1, grid=(bd, nj), in_specs=in_specs,
        out_specs=pl.BlockSpec((1, 1, D_ATT), lambda b, j, pt: (b, 0, 0)),
        scratch_shapes=[pltpu.VMEM((N_HEADS, 1), F32), pltpu.VMEM((N_HEADS, 1), F32),
                        pltpu.VMEM((N_HEADS, D_ATT), F32), pltpu.VMEM((N_HEADS, 1), F32)])
    out = pl.pallas_call(
        functools.partial(_attn_sample_kernel, pp=pp, page=page),
        grid_spec=grid_spec, out_shape=jax.ShapeDtypeStruct((bd, 1, D_ATT), BF16),
        compiler_params=_cparams(2), name="attn_sample",
    )(page_table, q[:, None, :], k_new[:, None, :], v_new[:, None, :], logf_new[:, :N_HEADS, None], tri,
      *([cache_k] * pp), *([cache_v] * pp), *([cache_f] * pp))
    return out[:, 0, :]


def _layer_weights(l, tm, norm1_g, w_in, b_f, b_gate, q_gain, k_gain, pool_w, pool_scale, w_pool_up, w_att_up,
                   dw_w, dw_b, conv_ln_g, conv_ln_b, w_conv_out, w_out, norm2_g, w_mlp_up, w_mlp_down):
    d = w_in.shape[1]
    w = w_in[l]
    o_f = 3 * D_ATT
    o_pool = o_f + N_HEADS
    o_conv = o_pool + 256
    o_gate = o_conv + 512
    w_main = jnp.concatenate([w[:, :o_f], w[:, o_pool:o_gate], w[:, o_f:o_pool],
                              jnp.zeros((d, LANES - N_HEADS), F32)], axis=1).astype(BF16)
    grp = jnp.arange(256) // HEAD_DIM
    poolw = jnp.zeros((256, 256), F32)
    for g in range(4):
        poolw = poolw.at[g * 64:(g + 1) * 64, g * 64:(g + 1) * 64].set(pool_w[l, g])
    return dict(
        n1=norm1_g[l][None, :], w_main=w_main, w_gate=w[:, o_gate:].astype(BF16),
        bf=jnp.pad(b_f[l], (0, LANES - N_HEADS))[None, :], bg=b_gate[l],
        qg=jnp.tile(q_gain[l], N_HEADS)[None, :], kg=jnp.tile(k_gain[l], N_HEADS)[None, :],
        qg_col=jnp.tile(q_gain[l], N_HEADS)[:, None], kg_col=jnp.tile(k_gain[l], N_HEADS)[:, None],
        gsum=(grp[:, None] == grp[None, :]).astype(BF16),
        ltri=(jnp.arange(tm)[:, None] >= jnp.arange(tm)[None, :]).astype(BF16),
        poolw=poolw.astype(BF16), pscale=pool_scale[l][None, :],
        dww=jnp.pad(dw_w[l], ((0, CONV_HIST - CONV_WIDTH), (0, 0))), dwb=dw_b[l][None, :],
        lng=conv_ln_g[l][None, :], lnb=conv_ln_b[l][None, :],
        w_att_up=w_att_up[l].astype(BF16), w_pool_up=w_pool_up[l].astype(BF16),
        w_conv_out=w_conv_out[l].astype(BF16), w_out=w_out[l].astype(BF16),
        n2=norm2_g[l][None, :], w_up=w_mlp_up[l].astype(BF16), w_down=w_mlp_down[l].astype(BF16))


def kernel(x_prompt, x_sample, cache_k, cache_v, cache_logf, state_pool, state_conv, page_table,
           norm1_g, w_in, b_f, b_gate, q_gain, k_gain, pool_w, pool_scale, w_pool_up, w_att_up,
           dw_w, dw_b, conv_ln_g, conv_ln_b, w_conv_out, w_out, norm2_g, w_mlp_up, w_mlp_down):
    B, T, D = x_prompt.shape
    bd = x_sample.shape[0]
    depth = w_in.shape[0]
    n_pool_pages, page = cache_k.shape[1], cache_k.shape[2]
    n_pages = page_table.shape[1]
    past_len = n_pages * page
    tm = min(512, T)
    pp = min(32, n_pages)

    tri = (jnp.arange(page)[:, None] <= jnp.arange(page)[None, :]).astype(BF16)
    ck_flat = jnp.transpose(cache_k, (0, 1, 3, 4, 2)).reshape(depth, n_pool_pages, D_ATT, page)
    cv_flat = jnp.transpose(cache_v, (0, 1, 3, 4, 2)).reshape(depth, n_pool_pages, D_ATT, page)
    cf_flat = jnp.transpose(cache_logf, (0, 1, 3, 2))

    xp = x_prompt
    xs = x_sample.reshape(bd, D)
    outs = [[] for _ in range(7)]
    stacks = ()
    for l in range(depth):
        lw = _layer_weights(l, tm, norm1_g, w_in, b_f, b_gate, q_gain, k_gain, pool_w, pool_scale, w_pool_up,
                            w_att_up, dw_w, dw_b, conv_ln_g, conv_ln_b, w_conv_out, w_out, norm2_g,
                            w_mlp_up, w_mlp_down)
        qt, ka, vtb, kt_all, vt_all, ft_all, pm, ca, ptail, ctail = _inproj_prompt(xp, lw, tm, l, depth, stacks)
        stacks = (kt_all, vt_all, ft_all)
        att_t = _attn_prompt(qt, ka, vtb, tm)
        x1 = _merge(xp, att_t, pm, ca, lw, tm)
        xp = _mlp(x1.reshape(B * T, D), lw, tm).reshape(B, T, D)
        sp_t = jnp.transpose(state_pool[l], (1, 0, 2))
        sc_t = jnp.transpose(state_conv[l], (1, 0, 2))
        qs, ks, vs, fs, us, gs, pms, cas = _inproj_sample(xs, lw, sp_t, sc_t, past_len)
        att_s = _attn_sample(page_table, qs, ks, vs, fs, tri, ck_flat, cv_flat, cf_flat, l, pp)
        xs1 = _merge(xs[None], att_s.T[None], pms[None], cas[None], lw, bd)[0]
        xs = _mlp(xs1, lw, bd)

        outs[0].append(ptail[:, POOL_HIST - POOL_BUF:])
        outs[1].append(ctail[:, CONV_HIST - CONV_BUF:])
        outs[2].append(ks.reshape(bd, 1, N_HEADS, HEAD_DIM))
        outs[3].append(vs.reshape(bd, 1, N_HEADS, HEAD_DIM))
        outs[4].append(fs[:, None, :N_HEADS])
        outs[5].append(jnp.concatenate([state_pool[l][:, 1:], us[:, None, :]], axis=1))
        outs[6].append(jnp.concatenate([state_conv[l][:, 1:], gs[:, None, :]], axis=1))
    kt_all, vt_all, ft_all = stacks
    k_prompt = jnp.transpose(kt_all.reshape(depth, B, N_HEADS, HEAD_DIM, T), (0, 1, 4, 2, 3))
    v_prompt = jnp.transpose(vt_all.reshape(depth, B, N_HEADS, HEAD_DIM, T), (0, 1, 4, 2, 3))
    f_prompt = jnp.transpose(ft_all, (0, 1, 3, 2))
    st = [jnp.stack(o) for o in outs]
    return (xp, xs.reshape(bd, 1, D), k_prompt, v_prompt, f_prompt) + tuple(st)
```
